```python
import math, functools
import jax, jax.numpy as jnp
from jax import lax
import numpy as np

D_MODEL = 1024
BATCH = 32
SEQ = 2048
DEPTH = 2
DEC_BATCH = 128
DEC_SEQ = 8
PAST_LEN = 16384
PAGE_SIZE = 128

RET_HEADS = 4
RET_DK = 64
RET_DV = 128
RET_CHUNK = 128
CONV_WIDTH = 512
CONV_K = 3
MLA_HEADS = 8
MLA_Q_LORA = 384
MLA_KV_LORA = 256
MLA_NOPE = 64
MLA_ROPE = 32
MLA_V = 64
MLA_Q_BLOCK = 128
MLA_SCALE = (MLA_NOPE + MLA_ROPE) ** -0.5
HGRN_HEADS = 4
HGRN_DK = 128
HGRN_DV = 128
HGRN_CHUNK = 16
D_FF = 4 * D_MODEL
ROPE_THETA = 10000.0
NORM_EPS = 1e-6
N_BRANCH = 4
NEG_BIG = -1e30

RET_W = RET_HEADS * RET_DV
MLA_W = MLA_HEADS * MLA_V
HGRN_W = HGRN_HEADS * HGRN_DV
BRANCH_SIZES = (RET_W, CONV_WIDTH, MLA_W, HGRN_W)
BRANCH_W = sum(BRANCH_SIZES)
IN_SIZES = (
    RET_HEADS * RET_DK, RET_HEADS * RET_DK, RET_W, RET_W,
    CONV_WIDTH, CONV_WIDTH, CONV_WIDTH,
    MLA_Q_LORA, MLA_KV_LORA, MLA_ROPE,
    HGRN_HEADS * HGRN_DK, HGRN_HEADS * HGRN_DK, HGRN_W, HGRN_W,
    N_BRANCH * D_MODEL,
)
IN_COLS = sum(IN_SIZES)

kernel_name = "hybrid_ret_conv_mla_hgrn2_step"


def _split(a, sizes, axis=-1):
    out, off = [], 0
    for s in sizes:
        out.append(lax.slice_in_dim(a, off, off + s, axis=axis))
        off += s
    return out


def rmsnorm(x, g):
    xf = x.astype(jnp.float32)
    y = xf * lax.rsqrt(jnp.mean(xf * xf, axis=-1, keepdims=True) + NORM_EPS)
    return (y * g.astype(jnp.float32)).astype(x.dtype)


def rope(x, pos):
    half = x.shape[-1] // 2
    inv = ROPE_THETA ** (-jnp.arange(half, dtype=jnp.float32) / half)
    ang = pos.astype(jnp.float32)[:, None] * inv[None, :]
    cos = jnp.cos(ang)[:, None, :]
    sin = jnp.sin(ang)[:, None, :]
    x1, x2 = x[..., :half], x[..., half:]
    return jnp.concatenate([x1 * cos - x2 * sin, x1 * sin + x2 * cos], axis=-1).astype(x.dtype)


def retention_chunked(q, k, v, state):
    B, T, H, _ = q.shape
    dv = v.shape[-1]
    f32 = jnp.float32
    C = math.gcd(T, RET_CHUNK)
    n = T // C
    to_chunks = lambda a: a.astype(f32).reshape(B, n, C, H, a.shape[-1]).transpose(1, 0, 3, 2, 4)
    log_gamma = jnp.log1p(-jnp.exp2(-5.0 - jnp.arange(H, dtype=f32)))
    j = jnp.arange(C, dtype=f32)
    rel = j[:, None] - j[None, :]
    causal = rel >= 0
    decay = jnp.where(causal, jnp.exp(log_gamma[:, None, None] * jnp.where(causal, rel, 0.0)), 0.0)
    q_dec = jnp.exp(log_gamma[:, None] * (j + 1.0))[:, :, None]
    k_dec = jnp.exp(log_gamma[:, None] * (C - 1.0 - j))[:, :, None]
    c_dec = jnp.exp(log_gamma * C)[:, None, None]

    def step(S, blk):
        qb, kb, vb = blk
        att = jnp.einsum('bhtd,bhsd->bhts', qb, kb) * decay
        o = jnp.einsum('bhts,bhsv->bhtv', att, vb) + jnp.einsum('bhtd,bhdv->bhtv', qb * q_dec, S)
        S = S * c_dec + jnp.einsum('bhsd,bhsv->bhdv', kb * k_dec, vb)
        return S, o

    S, o = lax.scan(step, state.astype(f32), (to_chunks(q), to_chunks(k), to_chunks(v)))
    return o.transpose(1, 0, 3, 2, 4).reshape(B, T, H, dv), S


def hgrn2_chunked(q, k, v, log_f, state):
    B, T, H, _ = q.shape
    dv = v.shape[-1]
    f32 = jnp.float32
    C = math.gcd(T, HGRN_CHUNK)
    n = T // C
    to_chunks = lambda a: a.astype(f32).reshape(B, n, C, H, a.shape[-1]).transpose(1, 0, 3, 2, 4)
    causal = jnp.tril(jnp.ones((C, C), dtype=bool))[:, :, None]

    def step(S, blk):
        qb, kb, vb, gb = blk
        G = jnp.cumsum(gb, axis=2)
        o_inter = jnp.einsum('bhtk,bhkv->bhtv', qb * jnp.exp(G), S)
        diff = G[:, :, :, None, :] - G[:, :, None, :, :]
        w = jnp.where(causal, jnp.exp(jnp.where(causal, diff, 0.0)), 0.0)
        att = jnp.einsum('bhtsk,bhsk->bhts', w * qb[:, :, :, None, :], kb)
        o = o_inter + jnp.einsum('bhts,bhsv->bhtv', att, vb)
        G_end = G[:, :, -1:, :]
        S = S * jnp.exp(G_end[:, :, 0, :, None]) + jnp.einsum('bhsk,bhsv->bhkv', kb * jnp.exp(G_end - G), vb)
        return S, o

    S, o = lax.scan(step, state.astype(f32), (to_chunks(q), to_chunks(k), to_chunks(v), to_chunks(log_f)))
    return o.transpose(1, 0, 3, 2, 4).reshape(B, T, H, dv), S


def short_conv(b, c, xin, w, buf):
    T = xin.shape[1]
    u = c * xin
    full = jnp.concatenate([buf.astype(u.dtype), u], axis=1)
    y = full[:, 0:T] * w[0]
    for j in range(1, CONV_K):
        y = y + full[:, j:j + T] * w[j]
    return b * y, full[:, T:]


def mla_attend_prompt(q_nope, q_pe, ckv, kpe, w_uk, w_uv):
    B, T, H, _ = q_nope.shape
    k_nope = jnp.einsum('btc,chd->bthd', ckv, w_uk)
    v = jnp.einsum('btc,chd->bthd', ckv, w_uv)
    QB = math.gcd(T, MLA_Q_BLOCK)
    key_pos = jnp.arange(T)

    def block(i):
        qs = i * QB
        qn = lax.dynamic_slice_in_dim(q_nope, qs, QB, axis=1)
        qp = lax.dynamic_slice_in_dim(q_pe, qs, QB, axis=1)
        s = (jnp.einsum('bqhd,bkhd->bhqk', qn, k_nope)
             + jnp.einsum('bqhd,bkd->bhqk', qp, kpe)).astype(jnp.float32) * MLA_SCALE
        mask = (qs + jnp.arange(QB))[:, None] >= key_pos[None, :]
        p = jax.nn.softmax(jnp.where(mask, s, NEG_BIG), axis=-1)
        return jnp.einsum('bhqk,bkhd->bqhd', p.astype(v.dtype), v)

    o = lax.map(block, jnp.arange(T // QB))
    return o.transpose(1, 0, 2, 3, 4).reshape(B, T, H * MLA_V)


def mla_attend_sample(q_nope, q_pe, ckv, kpe, w_uk, w_uv, lat_pool, pe_pool, page_table, layer):
    B, T, H, _ = q_nope.shape
    f32 = jnp.float32
    q_lat = jnp.einsum('bqhd,chd->bqhc', q_nope, w_uk)

    def page_step(carry, phys):
        m, l, acc = carry
        c = lat_pool[layer, phys]
        r = pe_pool[layer, phys]
        s = (jnp.einsum('bqhc,bpc->bhqp', q_lat, c)
             + jnp.einsum('bqhd,bpd->bhqp', q_pe, r)).astype(f32) * MLA_SCALE
        m_new = jnp.maximum(m, jnp.max(s, axis=-1))
        alpha = jnp.exp(m - m_new)
        p = jnp.exp(s - m_new[..., None])
        l = l * alpha + jnp.sum(p, axis=-1)
        acc = acc * alpha[..., None] + jnp.einsum('bhqp,bpc->bhqc', p, c.astype(f32))
        return (m_new, l, acc), None

    init = (jnp.full((B, H, T), NEG_BIG, f32), jnp.zeros((B, H, T), f32),
            jnp.zeros((B, H, T, MLA_KV_LORA), f32))
    (m, l, acc), _ = lax.scan(page_step, init, page_table.T)
    s = (jnp.einsum('bqhc,bkc->bhqk', q_lat, ckv)
         + jnp.einsum('bqhd,bkd->bhqk', q_pe, kpe)).astype(f32) * MLA_SCALE
    s = jnp.where(jnp.tril(jnp.ones((T, T), dtype=bool)), s, NEG_BIG)
    m_fin = jnp.maximum(m, jnp.max(s, axis=-1))
    alpha = jnp.exp(m - m_fin)
    p = jnp.exp(s - m_fin[..., None])
    l = l * alpha + jnp.sum(p, axis=-1)
    acc = acc * alpha[..., None] + jnp.einsum('bhqk,bkc->bhqc', p, ckv.astype(f32))
    o_lat = acc / l[..., None]
    o = jnp.einsum('bhqc,chd->bqhd', o_lat.astype(ckv.dtype), w_uv)
    return o.reshape(B, T, H * MLA_V)


def _trunk_layer(h, pos, ret_state, conv_buf, hgrn_state, attend, lower_bound,
                 g_mix, w_in_l, conv_w_l, q_norm_l, w_uq_l, kv_norm_l, w_uk_l, w_uv_l,
                 hgrn_norm_l, w_bo_l, w_out_l, g_ffn_l, w_ff1_l, w_ff2_l):
    B, T, _ = h.shape
    dt = h.dtype
    f32 = jnp.float32
    hn = rmsnorm(h, g_mix)
    (rq, rk, rv, rg, cb, cc, cx, mcq, mckv, mkr, hq, hf, hi, hg, mg) = _split(hn @ w_in_l, IN_SIZES)

    q = rope(rq.reshape(B, T, RET_HEADS, RET_DK), pos)
    k = rope(rk.reshape(B, T, RET_HEADS, RET_DK), pos) * (RET_DK ** -0.5)
    o_ret, ret_new = retention_chunked(q, k, rv.reshape(B, T, RET_HEADS, RET_DV), ret_state)
    o_ret = o_ret * lax.rsqrt(jnp.mean(o_ret * o_ret, axis=-1, keepdims=True) + NORM_EPS)
    o_ret = (o_ret.reshape(B, T, RET_W) * jax.nn.silu(rg.astype(f32))).astype(dt)

    o_conv, conv_new = short_conv(cb, cc, cx, conv_w_l, conv_buf)

    qf = (rmsnorm(mcq, q_norm_l) @ w_uq_l).reshape(B, T, MLA_HEADS, MLA_NOPE + MLA_ROPE)
    q_nope = qf[..., :MLA_NOPE]
    q_pe = rope(qf[..., MLA_NOPE:], pos)
    ckv = rmsnorm(mckv, kv_norm_l)
    kpe = rope(mkr[:, :, None, :], pos)[:, :, 0, :]
    o_mla = attend(q_nope, q_pe, ckv, kpe, w_uk_l, w_uv_l).astype(dt)

    z = hf.astype(f32).reshape(B, T, HGRN_HEADS, HGRN_DK)
    lb = lower_bound.reshape(HGRN_HEADS, HGRN_DK)
    f_gate = lb + (1.0 - lb) * jax.nn.sigmoid(z)
    log_f = jnp.log(f_gate)
    k_h = 1.0 - f_gate
    q_h = jax.nn.silu(hq.astype(f32)).reshape(B, T, HGRN_HEADS, HGRN_DK)
    o_h, hgrn_new = hgrn2_chunked(q_h, k_h, hi.reshape(B, T, HGRN_HEADS, HGRN_DV), log_f, hgrn_state)
    o_h = rmsnorm(o_h, hgrn_norm_l)
    o_h = (o_h.reshape(B, T, HGRN_W) * jax.nn.silu(hg.astype(f32))).astype(dt)

    gates = jax.nn.sigmoid(mg.astype(f32)).reshape(B, T, N_BRANCH, D_MODEL)
    w_rows = _split(w_bo_l, BRANCH_SIZES, axis=0)
    branches = (o_ret, o_conv, o_mla, o_h)
    merged = jnp.zeros((B, T, D_MODEL), f32)
    for i in range(N_BRANCH):
        merged = merged + gates[:, :, i, :] * (branches[i] @ w_rows[i]).astype(f32)
    h = h + merged.astype(dt) @ w_out_l

    u = jnp.square(jax.nn.relu(rmsnorm(h, g_ffn_l) @ w_ff1_l))
    h = h + u @ w_ff2_l
    return h, ret_new.astype(dt), conv_new.astype(dt), hgrn_new.astype(dt), ckv, kpe


def setup_inputs(seed: int = 0) -> dict:
    key = jax.random.key(seed)
    ks = jax.random.split(key, 32)
    f32 = jnp.float32
    nrm = lambda k, shape, scale: jax.random.normal(k, shape, f32) * scale
    n_pages = PAST_LEN // PAGE_SIZE
    n_used = DEC_BATCH * n_pages
    n_phys = n_used + n_used // 4
    return {
        'x_prompt': nrm(ks[0], (BATCH, SEQ, D_MODEL), 1.0),
        'x_sample': nrm(ks[1], (DEC_BATCH, DEC_SEQ, D_MODEL), 1.0),
        'cache_mla_latent': nrm(ks[2], (DEPTH, n_phys, PAGE_SIZE, MLA_KV_LORA), 1.0),
        'cache_mla_rope': nrm(ks[3], (DEPTH, n_phys, PAGE_SIZE, MLA_ROPE), 1.0),
        'page_table': jax.random.permutation(ks[4], n_phys)[:n_used].reshape(DEC_BATCH, n_pages).astype(jnp.int32),
        'state_retention': nrm(ks[5], (DEPTH, DEC_BATCH, RET_HEADS, RET_DK, RET_DV), 1.0),
        'state_conv': nrm(ks[6], (DEPTH, DEC_BATCH, CONV_K - 1, CONV_WIDTH), 1.0),
        'state_hgrn': nrm(ks[7], (DEPTH, DEC_BATCH, HGRN_HEADS, HGRN_DK, HGRN_DV), 1.0),
        'norm_mix': 1.0 + nrm(ks[8], (DEPTH, D_MODEL), 0.02),
        'w_in': nrm(ks[9], (DEPTH, D_MODEL, IN_COLS), D_MODEL ** -0.5),
        'conv_w': nrm(ks[10], (DEPTH, CONV_K, CONV_WIDTH), CONV_K ** -0.5),
        'mla_q_norm': 1.0 + nrm(ks[11], (DEPTH, MLA_Q_LORA), 0.02),
        'mla_w_uq': nrm(ks[12], (DEPTH, MLA_Q_LORA, MLA_HEADS * (MLA_NOPE + MLA_ROPE)), MLA_Q_LORA ** -0.5),
        'mla_kv_norm': 1.0 + nrm(ks[13], (DEPTH, MLA_KV_LORA), 0.02),
        'mla_w_uk': nrm(ks[14], (DEPTH, MLA_KV_LORA, MLA_HEADS, MLA_NOPE), MLA_KV_LORA ** -0.5),
        'mla_w_uv': nrm(ks[15], (DEPTH, MLA_KV_LORA, MLA_HEADS, MLA_V), MLA_KV_LORA ** -0.5),
        'hgrn_lb_logits': nrm(ks[16], (DEPTH, HGRN_HEADS * HGRN_DK), 0.5),
        'hgrn_out_norm': 1.0 + nrm(ks[17], (DEPTH, HGRN_DV), 0.02),
        'w_branch_out': nrm(ks[18], (DEPTH, BRANCH_W, D_MODEL), RET_W ** -0.5),
        'w_out': nrm(ks[19], (DEPTH, D_MODEL, D_MODEL), D_MODEL ** -0.5),
        'norm_ffn': 1.0 + nrm(ks[20], (DEPTH, D_MODEL), 0.02),
        'w_ff1': nrm(ks[21], (DEPTH, D_MODEL, D_FF), D_MODEL ** -0.5),
        'w_ff2': nrm(ks[22], (DEPTH, D_FF, D_MODEL), D_FF ** -0.5),
        'norm_final': 1.0 + nrm(ks[23], (D_MODEL,), 0.02),
    }


def reference(x_prompt, x_sample, cache_mla_latent, cache_mla_rope, page_table,
              state_retention, state_conv, state_hgrn,
              norm_mix, w_in, conv_w, mla_q_norm, mla_w_uq, mla_kv_norm, mla_w_uk, mla_w_uv,
              hgrn_lb_logits, hgrn_out_norm, w_branch_out, w_out, norm_ffn, w_ff1, w_ff2, norm_final):
    f32 = jnp.float32
    Bp, Tp, _ = x_prompt.shape
    Bs, Ts, _ = x_sample.shape
    pos_p = jnp.arange(Tp, dtype=jnp.int32)
    pos_s = PAST_LEN + jnp.arange(Ts, dtype=jnp.int32)
    lb_p = jax.nn.softmax(hgrn_lb_logits.astype(f32), axis=0)
    lower_bounds = jnp.cumsum(lb_p, axis=0) - lb_p[0:1]

    hp, hs = x_prompt, x_sample
    lat_p, rope_p, lat_s, rope_s = [], [], [], []
    ret_p, ret_s, conv_p, conv_s, hg_p, hg_s = [], [], [], [], [], []
    for l in range(DEPTH):
        weights = (norm_mix[l], w_in[l], conv_w[l], mla_q_norm[l], mla_w_uq[l], mla_kv_norm[l],
                   mla_w_uk[l], mla_w_uv[l], hgrn_out_norm[l], w_branch_out[l], w_out[l],
                   norm_ffn[l], w_ff1[l], w_ff2[l])
        hp, rS, cB, hS, ckv, kpe = _trunk_layer(
            hp, pos_p,
            jnp.zeros((Bp, RET_HEADS, RET_DK, RET_DV), f32),
            jnp.zeros((Bp, CONV_K - 1, CONV_WIDTH), x_prompt.dtype),
            jnp.zeros((Bp, HGRN_HEADS, HGRN_DK, HGRN_DV), f32),
            mla_attend_prompt, lower_bounds[l], *weights)
        ret_p.append(rS); conv_p.append(cB); hg_p.append(hS); lat_p.append(ckv); rope_p.append(kpe)
        attend_s = functools.partial(mla_attend_sample, lat_pool=cache_mla_latent, pe_pool=cache_mla_rope,
                                     page_table=page_table, layer=l)
        hs, rS, cB, hS, ckv, kpe = _trunk_layer(
            hs, pos_s, state_retention[l], state_conv[l], state_hgrn[l],
            attend_s, lower_bounds[l], *weights)
        ret_s.append(rS); conv_s.append(cB); hg_s.append(hS); lat_s.append(ckv); rope_s.append(kpe)

    y_prompt = rmsnorm(hp, norm_final)
    y_sample = rmsnorm(hs, norm_final)
    return (y_prompt, y_sample,
            jnp.stack(lat_p), jnp.stack(rope_p), jnp.stack(lat_s), jnp.stack(rope_s),
            jnp.stack(ret_p), jnp.stack(ret_s), jnp.stack(conv_p), jnp.stack(conv_s),
            jnp.stack(hg_p), jnp.stack(hg_s))
```

```python
import functools
import math

import numpy as np
import jax
import jax.numpy as jnp
from jax import lax
from jax.experimental import pallas as pl
from jax.experimental.pallas import tpu as pltpu

F32 = jnp.float32
BF16 = jnp.bfloat16

D_MODEL = 1024
PAGE_SIZE = 128
RET_HEADS, RET_DK, RET_DV, RET_CHUNK = 4, 64, 128, 128
CONV_WIDTH, CONV_K = 512, 3
MLA_HEADS, MLA_Q_LORA, MLA_KV_LORA, MLA_NOPE, MLA_ROPE, MLA_V = 8, 384, 256, 64, 32, 64
MLA_SCALE = (MLA_NOPE + MLA_ROPE) ** -0.5
HGRN_HEADS, HGRN_DK, HGRN_DV = 4, 128, 128
D_FF = 4 * D_MODEL
ROPE_THETA = 10000.0
NORM_EPS = 1e-6
N_BRANCH = 4
NEG_BIG = -1e30

RET_W = RET_HEADS * RET_DV
MLA_W = MLA_HEADS * MLA_V
HGRN_W = HGRN_HEADS * HGRN_DV
IN_SIZES = (
    RET_HEADS * RET_DK, RET_HEADS * RET_DK, RET_W, RET_W,
    CONV_WIDTH, CONV_WIDTH, CONV_WIDTH,
    MLA_Q_LORA, MLA_KV_LORA, MLA_ROPE,
    HGRN_HEADS * HGRN_DK, HGRN_HEADS * HGRN_DK, HGRN_W, HGRN_W,
    N_BRANCH * D_MODEL,
)

LANES = 128
SUBLANES = 8
VMEM_LIMIT_BYTES = 56 * 1024 * 1024

GATE_W = N_BRANCH * D_MODEL
HG_BLOCK_W = 4 * HGRN_W
RET_BLOCK_W = 1536
CONV_BLOCK_W = 3 * CONV_WIDTH
MLA_BLOCK_W = 768
PACKED_W = GATE_W + HG_BLOCK_W + RET_BLOCK_W + CONV_BLOCK_W + MLA_BLOCK_W
MLA_HEAD_PAD = 128
HGRN_SUB = 16
KEY_PAD = 128


def _cparams(sem):
    return pltpu.CompilerParams(dimension_semantics=sem, vmem_limit_bytes=VMEM_LIMIT_BYTES)


def _rms(x, g):
    return x * lax.rsqrt(jnp.mean(x * x, axis=-1, keepdims=True) + NORM_EPS) * g


def _silu(x):
    return x * jax.nn.sigmoid(x)


def _dot(a, b):
    return jnp.dot(a, b, preferred_element_type=F32)


def _dot_nt(a, b):
    return lax.dot_general(a, b, (((1,), (1,)), ((), ())), preferred_element_type=F32)


def _dot_tn(a, b):
    return lax.dot_general(a, b, (((0,), (0,)), ((), ())), preferred_element_type=F32)


def _inproj_kernel(x_ref, g_ref, w_ref, o_ref, xn_ref):
    @pl.when(pl.program_id(1) == 0)
    def _():
        xn_ref[...] = _rms(x_ref[...], g_ref[...]).astype(BF16)

    o_ref[...] = _dot(xn_ref[...], w_ref[...])


def _inproj(x, g, w, *, tm, tn):
    n, d = x.shape
    c = w.shape[1]
    return pl.pallas_call(
        _inproj_kernel,
        out_shape=jax.ShapeDtypeStruct((n, c), F32),
        grid=(n // tm, c // tn),
        in_specs=[
            pl.BlockSpec((tm, d), lambda i, j: (i, 0)),
            pl.BlockSpec((1, d), lambda i, j: (0, 0)),
            pl.BlockSpec((d, tn), lambda i, j: (0, j)),
        ],
        out_specs=pl.BlockSpec((tm, tn), lambda i, j: (i, j)),
        scratch_shapes=[pltpu.VMEM((tm, d), BF16)],
        compiler_params=_cparams(("parallel", "arbitrary")),
        name="in_proj",
    )(x, g, w)


def _ret_kernel(x_ref, cos_ref, sin_ref, decay_ref, qdec_ref, kdec_ref, s0_ref,
                o_ref, sout_ref, s_ref, *, nt, cdec):
    t = pl.program_id(1)
    hw = RET_HEADS * RET_DK // 2
    per = RET_DK // 2

    @pl.when(t == 0)
    def _():
        s_ref[...] = jnp.zeros(s_ref.shape, F32)
        for h in range(RET_HEADS):
            s_ref[h, h * per:(h + 1) * per, :] = s0_ref[0, h, 0:per, :]
            s_ref[h, hw + h * per:hw + (h + 1) * per, :] = s0_ref[0, h, per:2 * per, :]

    x = x_ref[...]
    c = x.shape[0]
    cos = cos_ref[...]
    sin = sin_ref[...]
    q1, q2 = x[:, 0:hw], x[:, hw:2 * hw]
    k1, k2 = x[:, 2 * hw:3 * hw], x[:, 3 * hw:4 * hw]
    qr = jnp.concatenate([q1 * cos - q2 * sin, q1 * sin + q2 * cos], axis=1)
    kr = jnp.concatenate([k1 * cos - k2 * sin, k1 * sin + k2 * cos], axis=1) * (RET_DK ** -0.5)
    v = x[:, 4 * hw:4 * hw + RET_W]
    g = x[:, 4 * hw + RET_W:4 * hw + 2 * RET_W]
    if c < KEY_PAD:
        kr = jnp.concatenate([kr, jnp.zeros((KEY_PAD - c, kr.shape[1]), F32)], axis=0)
        v = jnp.concatenate([v, jnp.zeros((KEY_PAD - c, v.shape[1]), F32)], axis=0)
    kt = kr.T
    ktb = kt.astype(BF16)
    lane = lax.broadcasted_iota(jnp.int32, (1, 2 * hw), 1)
    head_of_lane = (lane % hw) // per
    for h in range(RET_HEADS):
        qm = jnp.where(head_of_lane == h, qr, 0.0).astype(BF16)
        vh = v[:, h * RET_DV:(h + 1) * RET_DV].astype(BF16)
        sh = s_ref[h]
        att = _dot(qm, ktb) * decay_ref[h]
        o = _dot(att.astype(BF16), vh) + _dot(qm, sh.astype(BF16)) * qdec_ref[h]
        s_ref[h] = sh * cdec[h] + _dot((kt * kdec_ref[h]).astype(BF16), vh)
        o = o * lax.rsqrt(jnp.mean(o * o, axis=-1, keepdims=True) + NORM_EPS)
        gh = g[:, h * RET_DV:(h + 1) * RET_DV]
        o_ref[:, h * RET_DV:(h + 1) * RET_DV] = o * _silu(gh)

    @pl.when(t == nt - 1)
    def _():
        for h in range(RET_HEADS):
            sout_ref[0, h, 0:per, :] = s_ref[h, h * per:(h + 1) * per, :]
            sout_ref[0, h, per:2 * per, :] = s_ref[h, hw + h * per:hw + (h + 1) * per, :]


def _retention(proj, cos, sin, state, *, b, t):
    c = math.gcd(t, RET_CHUNK)
    nt = t // c
    ck = max(c, KEY_PAD)
    hw = RET_HEADS * RET_DK // 2
    hs = np.arange(RET_HEADS, dtype=np.float64)
    log_gamma = np.log1p(-np.exp2(-5.0 - hs))
    j = np.arange(c, dtype=np.float64)
    rel = j[:, None] - j[None, :]
    decay = np.where(rel >= 0, np.exp(log_gamma[:, None, None] * np.where(rel >= 0, rel, 0.0)), 0.0)
    decay_p = np.zeros((RET_HEADS, c, ck))
    decay_p[:, :, :c] = decay
    qdec = np.repeat(np.exp(log_gamma[:, None] * (j + 1.0))[:, :, None], RET_DV, axis=2)
    kdec = np.zeros((RET_HEADS, 1, ck))
    kdec[:, 0, :c] = np.exp(log_gamma[:, None] * (c - 1.0 - j))
    cdec = tuple(float(np.float32(np.exp(lg * c))) for lg in log_gamma)
    kern = functools.partial(_ret_kernel, nt=nt, cdec=cdec)
    return pl.pallas_call(
        kern,
        out_shape=(jax.ShapeDtypeStruct((b * t, RET_W), F32),
                   jax.ShapeDtypeStruct((b, RET_HEADS, RET_DK, RET_DV), F32)),
        grid=(b, nt),
        in_specs=[
            pl.BlockSpec((c, RET_BLOCK_W), lambda i, k: (i * nt + k, 4)),
            pl.BlockSpec((c, hw), lambda i, k: (k, 0)),
            pl.BlockSpec((c, hw), lambda i, k: (k, 0)),
            pl.BlockSpec((RET_HEADS, c, ck), lambda i, k: (0, 0, 0)),
            pl.BlockSpec((RET_HEADS, c, RET_DV), lambda i, k: (0, 0, 0)),
            pl.BlockSpec((RET_HEADS, 1, ck), lambda i, k: (0, 0, 0)),
            pl.BlockSpec((1, RET_HEADS, RET_DK, RET_DV), lambda i, k: (i, 0, 0, 0)),
        ],
        out_specs=(pl.BlockSpec((c, RET_W), lambda i, k: (i * nt + k, 0)),
                   pl.BlockSpec((1, RET_HEADS, RET_DK, RET_DV), lambda i, k: (i, 0, 0, 0))),
        scratch_shapes=[pltpu.VMEM((RET_HEADS, RET_HEADS * RET_DK, RET_DV), F32)],
        compiler_params=_cparams(("parallel", "arbitrary")),
        name="retention",
    )(proj, cos, sin, jnp.asarray(decay_p, F32), jnp.asarray(qdec, F32), jnp.asarray(kdec, F32), state)


def _conv_kernel(x_ref, w_ref, buf_ref, o_ref, bout_ref, carry_ref, *, nt):
    t = pl.program_id(1)
    w = CONV_WIDTH

    @pl.when(t == 0)
    def _():
        carry_ref[...] = jnp.zeros(carry_ref.shape, F32)
        carry_ref[SUBLANES - 2:SUBLANES, :] = buf_ref[0]

    x = x_ref[...]
    rows = x.shape[0]
    gate_b, gate_c, xin = x[:, 0:w], x[:, w:2 * w], x[:, 2 * w:3 * w]
    u = gate_c * xin
    row = lax.broadcasted_iota(jnp.int32, (rows, 1), 0)
    p1 = carry_ref[SUBLANES - 1:SUBLANES, :]
    p2 = carry_ref[SUBLANES - 2:SUBLANES - 1, :]
    u1 = jnp.where(row == 0, p1, pltpu.roll(u, 1, 0))
    u2 = jnp.where(row == 0, p2, jnp.where(row == 1, p1, pltpu.roll(u, 2, 0)))
    y = u2 * w_ref[0:1, :] + u1 * w_ref[1:2, :] + u * w_ref[2:3, :]
    o_ref[...] = gate_b * y
    carry_ref[...] = u[rows - SUBLANES:rows, :]

    @pl.when(t == nt - 1)
    def _():
        bout_ref[0] = u[rows - 2:rows, :]


def _short_conv(proj, w, buf, *, b, t, rows):
    nt = t // rows
    kern = functools.partial(_conv_kernel, nt=nt)
    return pl.pallas_call(
        kern,
        out_shape=(jax.ShapeDtypeStruct((b * t, CONV_WIDTH), F32),
                   jax.ShapeDtypeStruct((b, CONV_K - 1, CONV_WIDTH), F32)),
        grid=(b, nt),
        in_specs=[
            pl.BlockSpec((rows, CONV_BLOCK_W), lambda i, k: (i * nt + k, 5)),
            pl.BlockSpec((CONV_K, CONV_WIDTH), lambda i, k: (0, 0)),
            pl.BlockSpec((1, CONV_K - 1, CONV_WIDTH), lambda i, k: (i, 0, 0)),
        ],
        out_specs=(pl.BlockSpec((rows, CONV_WIDTH), lambda i, k: (i * nt + k, 0)),
                   pl.BlockSpec((1, CONV_K - 1, CONV_WIDTH), lambda i, k: (i, 0, 0))),
        scratch_shapes=[pltpu.VMEM((SUBLANES, CONV_WIDTH), F32)],
        compiler_params=_cparams(("parallel", "arbitrary")),
        name="short_conv",
    )(proj, w, buf)


def _split3(x):
    hi = x.astype(BF16)
    r = x - hi.astype(F32)
    mid = r.astype(BF16)
    lo = (r - mid.astype(F32)).astype(BF16)
    return hi, mid, lo


def _hgrn_kernel(x_ref, lb_ref, gn_ref, s0_ref, o_ref, sout_ref,
                 st_ref, g_ref, q_ref, k_ref, *, nt, valid_rows):
    t = pl.program_id(1)
    w = HGRN_W
    dk = HGRN_DK

    @pl.when(t == 0)
    def _():
        for h in range(HGRN_HEADS):
            st_ref[h] = s0_ref[0, h].T

    x = x_ref[...]
    rows = x.shape[0]
    ell = max(rows, HGRN_SUB)
    if rows < ell:
        x = jnp.concatenate([x, jnp.zeros((ell - rows, x.shape[1]), F32)], axis=0)
    hq, hf, hi, hg = x[:, 0:w], x[:, w:2 * w], x[:, 2 * w:3 * w], x[:, 3 * w:4 * w]
    lb = lb_ref[...]
    f_gate = lb + (1.0 - lb) * jax.nn.sigmoid(hf)
    log_f = jnp.log(f_gate)
    kk = 1.0 - f_gate
    if valid_rows < ell:
        ok = lax.broadcasted_iota(jnp.int32, (ell, 1), 0) < valid_rows
        log_f = jnp.where(ok, log_f, 0.0)
        kk = jnp.where(ok, kk, 0.0)
    qh = _silu(hq)

    ri = lax.broadcasted_iota(jnp.int32, (ell, ell), 0)
    ci = lax.broadcasted_iota(jnp.int32, (ell, ell), 1)
    tri = jnp.where(ri >= ci, 1.0, 0.0).astype(BF16)
    p_hi, p_mid, p_lo = _split3(log_f)
    gcum = _dot(tri, p_hi) + _dot(tri, p_mid) + _dot(tri, p_lo)

    g_ref[0:ell, :] = gcum
    q_ref[0:ell, :] = qh
    k_ref[0:ell, :] = kk
    g_end = gcum[ell - 1:ell, :]
    q_in = (qh * jnp.exp(gcum)).astype(BF16)
    k_out = (kk * jnp.exp(g_end - gcum)).astype(BF16)
    e_end = jnp.exp(g_end)

    nsub = ell // HGRN_SUB
    lane = lax.broadcasted_iota(jnp.int32, (1, KEY_PAD), 1)
    key_row = lax.broadcasted_iota(jnp.int32, (ell, 1), 0)
    sub_row = lax.broadcasted_iota(jnp.int32, (HGRN_SUB, 1), 0)

    a_blocks = [[None] * nsub for _ in range(HGRN_HEADS)]
    for j in range(nsub):
        r0 = j * HGRN_SUB
        gq = gcum[r0:r0 + HGRN_SUB, :]
        qq = qh[r0:r0 + HGRN_SUB, :]

        def diag_body(s, carry, r0=r0, gq=gq, qq=qq):
            gs = g_ref[pl.ds(r0 + s, 1), :]
            ks = k_ref[pl.ds(r0 + s, 1), :]
            valid = sub_row >= s
            p = jnp.where(valid, jnp.exp(jnp.minimum(gq - gs, 0.0)) * qq * ks, 0.0)
            out = []
            for h in range(HGRN_HEADS):
                a = jnp.sum(p[:, h * dk:(h + 1) * dk], axis=-1, keepdims=True)
                out.append(carry[h] + jnp.where(lane == r0 + s, a, 0.0))
            return tuple(out)

        init = tuple(jnp.zeros((HGRN_SUB, KEY_PAD), F32) for _ in range(HGRN_HEADS))
        diag = lax.fori_loop(0, HGRN_SUB, diag_body, init)
        if j > 0:
            gb = gcum[r0 - 1:r0, :]
            q_t = (qq * jnp.exp(gq - gb)).astype(BF16)
            k_t = jnp.where(key_row < r0, kk * jnp.exp(jnp.minimum(gb - gcum, 0.0)), 0.0)
            if ell < KEY_PAD:
                k_t = jnp.concatenate([k_t, jnp.zeros((KEY_PAD - ell, k_t.shape[1]), F32)], axis=0)
            k_t = k_t.astype(BF16)
        for h in range(HGRN_HEADS):
            blk = diag[h]
            if j > 0:
                blk = blk + _dot_nt(q_t[:, h * dk:(h + 1) * dk], k_t[:, h * dk:(h + 1) * dk])
            a_blocks[h][j] = blk

    gn = gn_ref[...]
    for h in range(HGRN_HEADS):
        sl = slice(h * dk, (h + 1) * dk)
        a_h = a_blocks[h][0] if nsub == 1 else jnp.concatenate(a_blocks[h], axis=0)
        vh = hi[:, h * HGRN_DV:(h + 1) * HGRN_DV]
        if ell < KEY_PAD:
            v_pad = jnp.concatenate([vh, jnp.zeros((KEY_PAD - ell, HGRN_DV), F32)], axis=0)
        else:
            v_pad = vh
        st = st_ref[h]
        o = _dot(a_h.astype(BF16), v_pad.astype(BF16)) + _dot_nt(q_in[:, sl], st.astype(BF16))
        st_ref[h] = st * e_end[:, sl] + _dot_tn(vh.astype(BF16), k_out[:, sl])
        o = _rms(o, gn) * _silu(hg[:, h * HGRN_DV:(h + 1) * HGRN_DV])
        o_ref[:, h * HGRN_DV:(h + 1) * HGRN_DV] = o[0:rows, :]

    @pl.when(t == nt - 1)
    def _():
        for h in range(HGRN_HEADS):
            sout_ref[0, h] = st_ref[h].T


def _hgrn(proj, lb, gnorm, state, *, b, t, rows):
    nt = t // rows
    kern = functools.partial(_hgrn_kernel, nt=nt, valid_rows=rows)
    ell = max(rows, HGRN_SUB)
    return pl.pallas_call(
        kern,
        out_shape=(jax.ShapeDtypeStruct((b * t, HGRN_W), F32),
                   jax.ShapeDtypeStruct((b, HGRN_HEADS, HGRN_DK, HGRN_DV), F32)),
        grid=(b, nt),
        in_specs=[
            pl.BlockSpec((rows, HG_BLOCK_W), lambda i, k: (i * nt + k, 2)),
            pl.BlockSpec((1, HGRN_W), lambda i, k: (0, 0)),
            pl.BlockSpec((1, HGRN_DV), lambda i, k: (0, 0)),
            pl.BlockSpec((1, HGRN_HEADS, HGRN_DK, HGRN_DV), lambda i, k: (i, 0, 0, 0)),
        ],
        out_specs=(pl.BlockSpec((rows, HGRN_W), lambda i, k: (i * nt + k, 0)),
                   pl.BlockSpec((1, HGRN_HEADS, HGRN_DK, HGRN_DV), lambda i, k: (i, 0, 0, 0))),
        scratch_shapes=[pltpu.VMEM((HGRN_HEADS, HGRN_DV, HGRN_DK), F32),
                        pltpu.VMEM((ell, HGRN_W), F32),
                        pltpu.VMEM((ell, HGRN_W), F32),
                        pltpu.VMEM((ell, HGRN_W), F32)],
        compiler_params=_cparams(("parallel", "arbitrary")),
        name="hgrn2",
    )(proj, lb, gnorm, state)


def _mla_prep_kernel(x_ref, qn_ref, kvn_ref, wqn_ref, wqp_ref, wqr_ref, eq_ref,
                     cq_ref, sq_ref, ck_ref, sk_ref, *rest, with_kv):
    if with_kv:
        wuk_ref, ek_ref, wuv_ref, q_ref, lat_ref, kpe_ref, kc_ref, v_ref = rest
    else:
        q_ref, lat_ref, kpe_ref = rest
    x = x_ref[...]
    mcq = x[:, 0:MLA_Q_LORA]
    mckv = x[:, MLA_Q_LORA:MLA_Q_LORA + MLA_KV_LORA]
    o = MLA_Q_LORA + MLA_KV_LORA
    mkr = x[:, o:o + MLA_ROPE]
    mkr_rot = x[:, o + MLA_ROPE:o + 2 * MLA_ROPE]
    xq = _rms(mcq, qn_ref[...]).astype(BF16)
    q_pe = _dot(xq, wqp_ref[...]) * cq_ref[...] + _dot(xq, wqr_ref[...]) * sq_ref[...]
    q_cat = _dot(xq, wqn_ref[...]) + _dot(q_pe.astype(BF16), eq_ref[...])
    q_ref[...] = q_cat.astype(q_ref.dtype)
    ckv = _rms(mckv, kvn_ref[...])
    lat_ref[...] = ckv
    kpe = mkr * ck_ref[...] + mkr_rot * sk_ref[...]
    kpe_ref[...] = kpe
    if with_kv:
        cb = ckv.astype(BF16)
        kc_ref[...] = (_dot(cb, wuk_ref[...]) + _dot(kpe.astype(BF16), ek_ref[...])).astype(BF16)
        v_ref[...] = _dot(cb, wuv_ref[...]).astype(BF16)


def _mla_prep(proj, lw, tabs, *, n, t, tm, with_kv):
    rows = min(tm, t)
    nt = t // rows
    hp = MLA_HEADS * MLA_HEAD_PAD
    qp_w = MLA_HEADS * MLA_ROPE
    full = lambda r, c: pl.BlockSpec((r, c), lambda i: (0, 0))
    tab = lambda c: pl.BlockSpec((rows, c), lambda i: (i % nt, 0))
    in_specs = [
        pl.BlockSpec((rows, MLA_BLOCK_W), lambda i: (i, 12)),
        full(1, MLA_Q_LORA), full(1, MLA_KV_LORA),
        full(MLA_Q_LORA, hp), full(MLA_Q_LORA, qp_w), full(MLA_Q_LORA, qp_w), full(qp_w, hp),
        tab(qp_w), tab(qp_w), tab(MLA_ROPE), tab(MLA_ROPE),
    ]
    args = [proj, lw["q_norm"], lw["kv_norm"], lw["wq_nope"], lw["wq_pe"], lw["wq_rot"], lw["e_q"],
            tabs["cos_q"], tabs["sin_q"], tabs["cos_k"], tabs["sin_k"]]
    q_dtype = BF16 if with_kv else F32
    out_shape = [jax.ShapeDtypeStruct((n, hp), q_dtype),
                 jax.ShapeDtypeStruct((n, MLA_KV_LORA), F32),
                 jax.ShapeDtypeStruct((n, MLA_ROPE), F32)]
    out_specs = [pl.BlockSpec((rows, hp), lambda i: (i, 0)),
                 pl.BlockSpec((rows, MLA_KV_LORA), lambda i: (i, 0)),
                 pl.BlockSpec((rows, MLA_ROPE), lambda i: (i, 0))]
    if with_kv:
        in_specs += [full(MLA_KV_LORA, hp), full(MLA_ROPE, hp), full(MLA_KV_LORA, MLA_W)]
        args += [lw["wuk_pad"], lw["e_k"], lw["wuv"]]
        out_shape += [jax.ShapeDtypeStruct((n, hp), BF16), jax.ShapeDtypeStruct((n, MLA_W), BF16)]
        out_specs += [pl.BlockSpec((rows, hp), lambda i: (i, 0)), pl.BlockSpec((rows, MLA_W), lambda i: (i, 0))]
    return pl.pallas_call(
        functools.partial(_mla_prep_kernel, with_kv=with_kv),
        out_shape=tuple(out_shape),
        grid=(n // rows,),
        in_specs=in_specs,
        out_specs=tuple(out_specs),
        compiler_params=_cparams(("parallel",)),
        name="mla_prep",
    )(*args)


def _flash_kernel(q_ref, k_ref, v_ref, o_ref, m_ref, l_ref, acc_ref, *, tq, tk):
    qi = pl.program_id(1)
    kj = pl.program_id(2)
    hp = MLA_HEAD_PAD

    @pl.when(kj == 0)
    def _():
        m_ref[...] = jnp.full(m_ref.shape, NEG_BIG, F32)
        l_ref[...] = jnp.zeros(l_ref.shape, F32)
        acc_ref[...] = jnp.zeros(acc_ref.shape, F32)

    @pl.when(kj <= qi)
    def _():
        qpos = qi * tq + lax.broadcasted_iota(jnp.int32, (tq, tk), 0)
        kpos = kj * tk + lax.broadcasted_iota(jnp.int32, (tq, tk), 1)
        mask = qpos >= kpos
        for h in range(MLA_HEADS):
            q = q_ref[:, h * hp:(h + 1) * hp]
            k = k_ref[:, h * hp:(h + 1) * hp]
            s = _dot_nt(q, k) * MLA_SCALE
            s = jnp.where(mask, s, NEG_BIG)
            m_old = m_ref[h]
            m_new = jnp.maximum(m_old, jnp.max(s, axis=-1, keepdims=True))
            alpha = jnp.exp(m_old - m_new)
            p = jnp.exp(s - m_new)
            l_ref[h] = l_ref[h] * alpha + jnp.sum(p, axis=-1, keepdims=True)
            pair = h // 2
            vp = v_ref[:, pair * hp:(pair + 1) * hp]
            acc_ref[h] = acc_ref[h] * alpha + _dot(p.astype(BF16), vp)
            m_ref[h] = m_new

    @pl.when(kj == qi)
    def _():
        lane = lax.broadcasted_iota(jnp.int32, (1, hp), 1)
        for pair in range(MLA_HEADS // 2):
            a0 = acc_ref[2 * pair] / l_ref[2 * pair]
            a1 = acc_ref[2 * pair + 1] / l_ref[2 * pair + 1]
            o_ref[:, pair * hp:(pair + 1) * hp] = jnp.where(lane < MLA_V, a0, a1)


def _flash(q, k, v, *, b, t, tq, tk):
    nq, nk = t // tq, t // tk
    hp = MLA_HEADS * MLA_HEAD_PAD
    kern = functools.partial(_flash_kernel, tq=tq, tk=tk)
    return pl.pallas_call(
        kern,
        out_shape=jax.ShapeDtypeStruct((b * t, MLA_W), F32),
        grid=(b, nq, nk),
        in_specs=[
            pl.BlockSpec((tq, hp), lambda i, a, c: (i * nq + a, 0)),
            pl.BlockSpec((tk, hp), lambda i, a, c: (i * nk + jnp.minimum(c, a), 0)),
            pl.BlockSpec((tk, MLA_W), lambda i, a, c: (i * nk + jnp.minimum(c, a), 0)),
        ],
        out_specs=pl.BlockSpec((tq, MLA_W), lambda i, a, c: (i * nq + a, 0)),
        scratch_shapes=[pltpu.VMEM((MLA_HEADS, tq, 1), F32),
                        pltpu.VMEM((MLA_HEADS, tq, 1), F32),
                        pltpu.VMEM((MLA_HEADS, tq, MLA_HEAD_PAD), F32)],
        compiler_params=_cparams(("parallel", "parallel", "arbitrary")),
        name="mla_flash",
    )(q, k, v)


def _paged_kernel(pt_ref, q_ref, lat_new_ref, kpe_new_ref, wabs_ref, wuv_ref, *rest,
                  ng, pg, t):
    lat_refs = rest[0:pg]
    pe_refs = rest[pg:2 * pg]
    o_ref = rest[2 * pg]
    ql_ref, qp_ref, m_ref, l_ref, acc_ref = rest[2 * pg + 1:]
    g = pl.program_id(1)
    hp = MLA_HEAD_PAD
    rows = MLA_HEADS * t

    @pl.when(g == 0)
    def _():
        q = q_ref[0]
        lane_head = lax.broadcasted_iota(jnp.int32, (1, MLA_HEADS * hp), 1) // hp
        q_exp = jnp.concatenate(
            [jnp.where(lane_head == h, q, 0.0) for h in range(MLA_HEADS)], axis=0).astype(BF16)
        q_abs = _dot(q_exp, wabs_ref[...])
        ql_ref[...] = q_abs[:, 0:MLA_KV_LORA].astype(BF16)
        qp_ref[...] = q_abs[:, MLA_KV_LORA:MLA_KV_LORA + LANES].astype(BF16)
        m_ref[...] = jnp.full(m_ref.shape, NEG_BIG, F32)
        l_ref[...] = jnp.zeros(l_ref.shape, F32)
        acc_ref[...] = jnp.zeros(acc_ref.shape, F32)

    ql = ql_ref[...]
    qp = qp_ref[:, 0:MLA_ROPE]

    def update(s_list, c_list):
        s = jnp.concatenate(s_list, axis=1) if len(s_list) > 1 else s_list[0]
        m_old = m_ref[...]
        m_new = jnp.maximum(m_old, jnp.max(s, axis=-1, keepdims=True))
        alpha = jnp.exp(m_old - m_new)
        p = jnp.exp(s - m_new)
        l_ref[...] = l_ref[...] * alpha + jnp.sum(p, axis=-1, keepdims=True)
        acc = acc_ref[...] * alpha
        for i, cb in enumerate(c_list):
            acc = acc + _dot(p[:, i * PAGE_SIZE:(i + 1) * PAGE_SIZE].astype(BF16), cb)
        acc_ref[...] = acc
        m_ref[...] = m_new

    s_list, c_list = [], []
    for i in range(pg):
        cb = lat_refs[i][0, 0].astype(BF16)
        rb = pe_refs[i][0, 0].astype(BF16)
        s_list.append((_dot_nt(ql, cb) + _dot_nt(qp, rb)) * MLA_SCALE)
        c_list.append(cb)
    update(s_list, c_list)

    @pl.when(g == ng - 1)
    def _():
        cn = jnp.concatenate([lat_new_ref[0], jnp.zeros((KEY_PAD - t, MLA_KV_LORA), F32)], axis=0).astype(BF16)
        rn = jnp.concatenate([kpe_new_ref[0], jnp.zeros((KEY_PAD - t, MLA_ROPE), F32)], axis=0).astype(BF16)
        s = (_dot_nt(ql, cn) + _dot_nt(qp, rn)) * MLA_SCALE
        qtok = lax.broadcasted_iota(jnp.int32, (rows, KEY_PAD), 0) % t
        kidx = lax.broadcasted_iota(jnp.int32, (rows, KEY_PAD), 1)
        s = jnp.where(kidx <= qtok, s, NEG_BIG)
        update([s], [cn])
        o_lat = acc_ref[...] / l_ref[...]
        res = _dot(o_lat.astype(BF16), wuv_ref[...])
        lane_head = lax.broadcasted_iota(jnp.int32, (1, MLA_W), 1) // MLA_V
        out = jnp.zeros((t, MLA_W), F32)
        for h in range(MLA_HEADS):
            out = out + jnp.where(lane_head == h, res[h * t:(h + 1) * t, :], 0.0)
        o_ref[0] = out


def _paged_attention(page_table, q, lat_new, kpe_new, wabs, wuv, lat_pool, pe_pool, *, layer, pg):
    b, t, hp = q.shape
    n_pages = page_table.shape[1]
    ng = n_pages // pg
    rows = MLA_HEADS * t

    def page_spec(width, i):
        return pl.BlockSpec((1, 1, PAGE_SIZE, width),
                            lambda bi, g, pt: (layer, pt[bi, g * pg + i], 0, 0))

    in_specs = [
        pl.BlockSpec((1, t, hp), lambda bi, g, pt: (bi, 0, 0)),
        pl.BlockSpec((1, t, MLA_KV_LORA), lambda bi, g, pt: (bi, 0, 0)),
        pl.BlockSpec((1, t, MLA_ROPE), lambda bi, g, pt: (bi, 0, 0)),
        pl.BlockSpec(wabs.shape, lambda bi, g, pt: (0, 0)),
        pl.BlockSpec(wuv.shape, lambda bi, g, pt: (0, 0)),
    ]
    in_specs += [page_spec(MLA_KV_LORA, i) for i in range(pg)]
    in_specs += [page_spec(MLA_ROPE, i) for i in range(pg)]
    kern = functools.partial(_paged_kernel, ng=ng, pg=pg, t=t)
    return pl.pallas_call(
        kern,
        out_shape=jax.ShapeDtypeStruct((b, t, MLA_W), F32),
        grid_spec=pltpu.PrefetchScalarGridSpec(
            num_scalar_prefetch=1,
            grid=(b, ng),
            in_specs=in_specs,
            out_specs=pl.BlockSpec((1, t, MLA_W), lambda bi, g, pt: (bi, 0, 0)),
            scratch_shapes=[pltpu.VMEM((rows, MLA_KV_LORA), BF16),
                            pltpu.VMEM((rows, LANES), BF16),
                            pltpu.VMEM((rows, 1), F32),
                            pltpu.VMEM((rows, 1), F32),
                            pltpu.VMEM((rows, MLA_KV_LORA), F32)],
        ),
        compiler_params=_cparams(("parallel", "arbitrary")),
        name="mla_paged",
    )(page_table, q, lat_new, kpe_new, wabs, wuv, *([lat_pool] * pg), *([pe_pool] * pg))


def _merge_kernel(h_ref, gate_ref, b0_ref, b1_ref, b2_ref, b3_ref, wbo_ref, wout_ref, o_ref):
    branches = (b0_ref, b1_ref, b2_ref, b3_ref)
    merged = None
    off = 0
    for i, br in enumerate(branches):
        wd = br.shape[1]
        proj = _dot(br[...].astype(BF16), wbo_ref[off:off + wd, :])
        term = jax.nn.sigmoid(gate_ref[:, i * D_MODEL:(i + 1) * D_MODEL]) * proj
        merged = term if merged is None else merged + term
        off += wd
    o_ref[...] = h_ref[...] + _dot(merged.astype(BF16), wout_ref[...])


def _merge(h, proj, branches, wbo, wout, *, tm):
    n = h.shape[0]
    row = lambda c: pl.BlockSpec((tm, c), lambda i: (i, 0))
    return pl.pallas_call(
        _merge_kernel,
        out_shape=jax.ShapeDtypeStruct((n, D_MODEL), F32),
        grid=(n // tm,),
        in_specs=[row(D_MODEL), row(GATE_W)] + [row(br.shape[1]) for br in branches]
        + [pl.BlockSpec(wbo.shape, lambda i: (0, 0)), pl.BlockSpec(wout.shape, lambda i: (0, 0))],
        out_specs=row(D_MODEL),
        compiler_params=_cparams(("parallel",)),
        name="branch_merge",
    )(h, proj, *branches, wbo, wout)


def _ffn_kernel(h_ref, g_ref, w1_ref, w2_ref, gf_ref, o_ref, xn_ref, acc_ref, *, nf, final_norm):
    j = pl.program_id(1)

    @pl.when(j == 0)
    def _():
        xn_ref[...] = _rms(h_ref[...], g_ref[...]).astype(BF16)
        acc_ref[...] = jnp.zeros(acc_ref.shape, F32)

    u = _dot(xn_ref[...], w1_ref[...])
    u = jnp.square(jnp.maximum(u, 0.0))
    acc_ref[...] += _dot(u.astype(BF16), w2_ref[...])

    @pl.when(j == nf - 1)
    def _():
        out = h_ref[...] + acc_ref[...]
        if final_norm:
            out = _rms(out, gf_ref[...])
        o_ref[...] = out


def _ffn(h, g, w1, w2, gf, *, tm, tf, final_norm):
    n, d = h.shape
    f = w1.shape[1]
    nf = f // tf
    kern = functools.partial(_ffn_kernel, nf=nf, final_norm=final_norm)
    return pl.pallas_call(
        kern,
        out_shape=jax.ShapeDtypeStruct((n, d), F32),
        grid=(n // tm, nf),
        in_specs=[
            pl.BlockSpec((tm, d), lambda i, j: (i, 0)),
            pl.BlockSpec((1, d), lambda i, j: (0, 0)),
            pl.BlockSpec((d, tf), lambda i, j: (0, j)),
            pl.BlockSpec((tf, d), lambda i, j: (j, 0)),
            pl.BlockSpec((1, d), lambda i, j: (0, 0)),
        ],
        out_specs=pl.BlockSpec((tm, d), lambda i, j: (i, 0)),
        scratch_shapes=[pltpu.VMEM((tm, d), BF16), pltpu.VMEM((tm, d), F32)],
        compiler_params=_cparams(("parallel", "arbitrary")),
        name="relu2_mlp",
    )(h, g, w1, w2, gf)


def _rot_partner(w, half):
    return jnp.concatenate([-w[..., half:], w[..., :half]], axis=-1)


def _pack_layer(w_in_l, conv_w_l, q_norm_l, w_uq_l, kv_norm_l, w_uk_l, w_uv_l, hgrn_norm_l,
                w_bo_l, w_out_l, g_mix_l, g_ffn_l, w_ff1_l, w_ff2_l):
    d = w_in_l.shape[0]
    offs = np.cumsum((0,) + IN_SIZES)
    col = lambda i: w_in_l[:, int(offs[i]):int(offs[i + 1])]
    rq, rk, rv, rg, cb, cc, cx, mcq, mckv, mkr, hq, hf, hi, hg, mg = [col(i) for i in range(15)]

    def halves(w):
        w4 = w.reshape(d, RET_HEADS, 2, RET_DK // 2)
        return w4[:, :, 0, :].reshape(d, -1), w4[:, :, 1, :].reshape(d, -1)

    rq1, rq2 = halves(rq)
    rk1, rk2 = halves(rk)
    mkr_rot = _rot_partner(mkr, MLA_ROPE // 2)
    pad = jnp.zeros((d, MLA_BLOCK_W - MLA_Q_LORA - MLA_KV_LORA - 2 * MLA_ROPE), w_in_l.dtype)
    packed = jnp.concatenate([mg, hq, hf, hi, hg, rq1, rq2, rk1, rk2, rv, rg, cb, cc, cx,
                              mcq, mckv, mkr, mkr_rot, pad], axis=1).astype(BF16)

    qd = MLA_NOPE + MLA_ROPE
    wq = w_uq_l.reshape(MLA_Q_LORA, MLA_HEADS, qd)
    wq_nope = jnp.concatenate(
        [wq[:, :, :MLA_NOPE], jnp.zeros((MLA_Q_LORA, MLA_HEADS, MLA_HEAD_PAD - MLA_NOPE), wq.dtype)],
        axis=2).reshape(MLA_Q_LORA, MLA_HEADS * MLA_HEAD_PAD)
    wq_pe = wq[:, :, MLA_NOPE:]
    wq_rot = _rot_partner(wq_pe, MLA_ROPE // 2)
    e_q = np.zeros((MLA_HEADS * MLA_ROPE, MLA_HEADS * MLA_HEAD_PAD), np.float32)
    e_k = np.zeros((MLA_ROPE, MLA_HEADS * MLA_HEAD_PAD), np.float32)
    for h in range(MLA_HEADS):
        for i in range(MLA_ROPE):
            e_q[h * MLA_ROPE + i, h * MLA_HEAD_PAD + MLA_NOPE + i] = 1.0
            e_k[i, h * MLA_HEAD_PAD + MLA_NOPE + i] = 1.0
    wuk_pad = jnp.concatenate(
        [w_uk_l, jnp.zeros((MLA_KV_LORA, MLA_HEADS, MLA_HEAD_PAD - MLA_NOPE), w_uk_l.dtype)],
        axis=2).reshape(MLA_KV_LORA, MLA_HEADS * MLA_HEAD_PAD)
    wabs = jnp.zeros((MLA_HEADS, MLA_HEAD_PAD, MLA_KV_LORA + LANES), F32)
    wabs = wabs.at[:, :MLA_NOPE, :MLA_KV_LORA].set(jnp.transpose(w_uk_l, (1, 2, 0)))
    wabs = wabs.at[:, MLA_NOPE:MLA_NOPE + MLA_ROPE, MLA_KV_LORA:MLA_KV_LORA + MLA_ROPE].set(
        jnp.broadcast_to(jnp.eye(MLA_ROPE, dtype=F32), (MLA_HEADS, MLA_ROPE, MLA_ROPE)))
    wabs = wabs.reshape(MLA_HEADS * MLA_HEAD_PAD, MLA_KV_LORA + LANES)
    return {
        "w_in": packed,
        "g_mix": g_mix_l.reshape(1, d),
        "conv_w": conv_w_l,
        "q_norm": q_norm_l.reshape(1, -1),
        "kv_norm": kv_norm_l.reshape(1, -1),
        "wq_nope": wq_nope.astype(BF16),
        "wq_pe": wq_pe.reshape(MLA_Q_LORA, -1).astype(BF16),
        "wq_rot": wq_rot.reshape(MLA_Q_LORA, -1).astype(BF16),
        "e_q": jnp.asarray(e_q, BF16),
        "e_k": jnp.asarray(e_k, BF16),
        "wuk_pad": wuk_pad.astype(BF16),
        "wuv": w_uv_l.reshape(MLA_KV_LORA, MLA_W).astype(BF16),
        "wabs": wabs.astype(BF16),
        "hgrn_norm": hgrn_norm_l.reshape(1, -1),
        "w_bo": w_bo_l.astype(BF16),
        "w_out": w_out_l.astype(BF16),
        "g_ffn": g_ffn_l.reshape(1, d),
        "w_ff1": w_ff1_l.astype(BF16),
        "w_ff2": w_ff2_l.astype(BF16),
    }


def _rope_tables(pos):
    def cs(half):
        inv = ROPE_THETA ** (-jnp.arange(half, dtype=F32) / half)
        ang = pos.astype(F32)[:, None] * inv[None, :]
        return jnp.cos(ang), jnp.sin(ang)

    c_ret, s_ret = cs(RET_DK // 2)
    c_pe, s_pe = cs(MLA_ROPE // 2)
    two = lambda a: jnp.concatenate([a, a], axis=1)
    return {
        "cos_ret": jnp.tile(c_ret, (1, RET_HEADS)), "sin_ret": jnp.tile(s_ret, (1, RET_HEADS)),
        "cos_q": jnp.tile(two(c_pe), (1, MLA_HEADS)), "sin_q": jnp.tile(two(s_pe), (1, MLA_HEADS)),
        "cos_k": two(c_pe), "sin_k": two(s_pe),
    }


def _tiles(t):
    return {"conv": min(512, t), "hgrn": min(64, t), "prep": min(512, t), "tq": min(512, t), "tk": min(512, t)}


def _trunk_layer(h, tabs, ret_state, conv_buf, hgrn_state, lb, lw, *, b, t, attend, final_gain):
    n = b * t
    tl = _tiles(t)
    big = min(n, 1024)
    proj = _inproj(h, lw["g_mix"], lw["w_in"], tm=big, tn=768)
    o_ret, ret_new = _retention(proj, tabs["cos_ret"], tabs["sin_ret"], ret_state, b=b, t=t)
    o_conv, conv_new = _short_conv(proj, lw["conv_w"], conv_buf, b=b, t=t, rows=tl["conv"])
    o_h, hgrn_new = _hgrn(proj, lb, lw["hgrn_norm"], hgrn_state, b=b, t=t, rows=tl["hgrn"])
    o_mla, ckv, kpe = attend(proj, lw, tabs, tl)
    h = _merge(h, proj, (o_ret, o_conv, o_mla, o_h), lw["w_bo"], lw["w_out"], tm=min(n, 256))
    gf = final_gain if final_gain is not None else lw["g_ffn"]
    h = _ffn(h, lw["g_ffn"], lw["w_ff1"], lw["w_ff2"], gf, tm=big, tf=512,
             final_norm=final_gain is not None)
    return h, ret_new, conv_new, hgrn_new, ckv, kpe


def kernel(x_prompt, x_sample, cache_mla_latent, cache_mla_rope, page_table,
           state_retention, state_conv, state_hgrn,
           norm_mix, w_in, conv_w, mla_q_norm, mla_w_uq, mla_kv_norm, mla_w_uk, mla_w_uv,
           hgrn_lb_logits, hgrn_out_norm, w_branch_out, w_out, norm_ffn, w_ff1, w_ff2, norm_final):
    bp, tp, d = x_prompt.shape
    bs, ts, _ = x_sample.shape
    depth = w_in.shape[0]
    past_len = page_table.shape[1] * PAGE_SIZE
    tabs_p = _rope_tables(jnp.arange(tp, dtype=jnp.int32))
    tabs_s = _rope_tables(past_len + jnp.arange(ts, dtype=jnp.int32))
    lb_p = jax.nn.softmax(hgrn_lb_logits.astype(F32), axis=0)
    lower_bounds = jnp.cumsum(lb_p, axis=0) - lb_p[0:1]
    gfin = norm_final.reshape(1, d)

    hp = x_prompt.reshape(bp * tp, d)
    hs = x_sample.reshape(bs * ts, d)
    zeros_ret = jnp.zeros((bp, RET_HEADS, RET_DK, RET_DV), F32)
    zeros_conv = jnp.zeros((bp, CONV_K - 1, CONV_WIDTH), F32)
    zeros_hgrn = jnp.zeros((bp, HGRN_HEADS, HGRN_DK, HGRN_DV), F32)
    outs = {k: [] for k in ("lat_p", "rope_p", "lat_s", "rope_s", "ret_p", "ret_s",
                            "conv_p", "conv_s", "hg_p", "hg_s")}
    for l in range(depth):
        lw = _pack_layer(w_in[l], conv_w[l], mla_q_norm[l], mla_w_uq[l], mla_kv_norm[l], mla_w_uk[l],
                         mla_w_uv[l], hgrn_out_norm[l], w_branch_out[l], w_out[l], norm_mix[l],
                         norm_ffn[l], w_ff1[l], w_ff2[l])
        lb = lower_bounds[l].reshape(1, HGRN_W)
        last = gfin if l == depth - 1 else None

        def attend_prompt(proj, lw, tabs, tl):
            q, ckv, kpe, kc, v = _mla_prep(proj, lw, tabs, n=bp * tp, t=tp, tm=tl["prep"], with_kv=True)
            o = _flash(q, kc, v, b=bp, t=tp, tq=tl["tq"], tk=tl["tk"])
            return o, ckv, kpe

        def attend_sample(proj, lw, tabs, tl, layer=l):
            q, ckv, kpe = _mla_prep(proj, lw, tabs, n=bs * ts, t=ts, tm=tl["prep"], with_kv=False)
            o = _paged_attention(page_table, q.reshape(bs, ts, -1), ckv.reshape(bs, ts, -1),
                                 kpe.reshape(bs, ts, -1), lw["wabs"], lw["wuv"],
                                 cache_mla_latent, cache_mla_rope, layer=layer,
                                 pg=math.gcd(16, page_table.shape[1]))
            return o.reshape(bs * ts, MLA_W), ckv, kpe

        hp, r, c, g, ckv, kpe = _trunk_layer(hp, tabs_p, zeros_ret, zeros_conv, zeros_hgrn, lb, lw,
                                             b=bp, t=tp, attend=attend_prompt, final_gain=last)
        outs["ret_p"].append(r); outs["conv_p"].append(c); outs["hg_p"].append(g)
        outs["lat_p"].append(ckv.reshape(bp, tp, -1)); outs["rope_p"].append(kpe.reshape(bp, tp, -1))
        hs, r, c, g, ckv, kpe = _trunk_layer(hs, tabs_s, state_retention[l], state_conv[l], state_hgrn[l],
                                             lb, lw, b=bs, t=ts, attend=attend_sample, final_gain=last)
        outs["ret_s"].append(r); outs["conv_s"].append(c); outs["hg_s"].append(g)
        outs["lat_s"].append(ckv.reshape(bs, ts, -1)); outs["rope_s"].append(kpe.reshape(bs, ts, -1))

    st = lambda k: jnp.stack(outs[k])
    return (hp.reshape(bp, tp, d), hs.reshape(bs, ts, d),
            st("lat_p"), st("rope_p"), st("lat_s"), st("rope_s"),
            st("ret_p"), st("ret_s"), st("conv_p"), st("conv_s"), st("hg_p"), st("hg_s"))
```

```python
import functools
import math

import numpy as np
import jax
import jax.numpy as jnp
from jax import lax
from jax.experimental import pallas as pl
from jax.experimental.pallas import tpu as pltpu

F32 = jnp.float32
BF16 = jnp.bfloat16

D_MODEL = 1024
PAGE_SIZE = 128
RET_HEADS, RET_DK, RET_DV, RET_CHUNK = 4, 64, 128, 128
CONV_WIDTH, CONV_K = 512, 3
MLA_HEADS, MLA_Q_LORA, MLA_KV_LORA, MLA_NOPE, MLA_ROPE, MLA_V = 8, 384, 256, 64, 32, 64
MLA_SCALE = (MLA_NOPE + MLA_ROPE) ** -0.5
HGRN_HEADS, HGRN_DK, HGRN_DV = 4, 128, 128
D_FF = 4 * D_MODEL
ROPE_THETA = 10000.0
NORM_EPS = 1e-6
N_BRANCH = 4
NEG_BIG = -1e30
LOG2E = math.log2(math.e)

RET_W = RET_HEADS * RET_DV
MLA_W = MLA_HEADS * MLA_V
HGRN_W = HGRN_HEADS * HGRN_DV
IN_SIZES = (
    RET_HEADS * RET_DK, RET_HEADS * RET_DK, RET_W, RET_W,
    CONV_WIDTH, CONV_WIDTH, CONV_WIDTH,
    MLA_Q_LORA, MLA_KV_LORA, MLA_ROPE,
    HGRN_HEADS * HGRN_DK, HGRN_HEADS * HGRN_DK, HGRN_W, HGRN_W,
    N_BRANCH * D_MODEL,
)

LANES = 128
SUBLANES = 8
VMEM_LIMIT_BYTES = 56 * 1024 * 1024

GATE_W = N_BRANCH * D_MODEL
HG_BLOCK_W = 4 * HGRN_W
RET_BLOCK_W = 1536
CONV_BLOCK_W = 3 * CONV_WIDTH
MLA_BLOCK_W = 768
PACKED_W = GATE_W + HG_BLOCK_W + RET_BLOCK_W + CONV_BLOCK_W + MLA_BLOCK_W
MLA_HEAD_PAD = 128
HGRN_SUB = 16
HGRN_CHUNK = 64
HGRN_SAFE_LOG_DECAY = 160.0
KEY_PAD = 128


def _cparams(sem):
    return pltpu.CompilerParams(dimension_semantics=sem, vmem_limit_bytes=VMEM_LIMIT_BYTES)


def _rms(x, g):
    return x * lax.rsqrt(jnp.mean(x * x, axis=-1, keepdims=True) + NORM_EPS) * g


def _silu(x):
    return x * jax.nn.sigmoid(x)


def _dot(a, b):
    return jnp.dot(a, b, preferred_element_type=F32)


def _dot_nt(a, b):
    return lax.dot_general(a, b, (((1,), (1,)), ((), ())), preferred_element_type=F32)


def _dot_tn(a, b):
    return lax.dot_general(a, b, (((0,), (0,)), ((), ())), preferred_element_type=F32)


def _inproj_kernel(x_ref, g_ref, w_ref, o_ref, xn_ref):
    @pl.when(pl.program_id(1) == 0)
    def _():
        xn_ref[...] = _rms(x_ref[...], g_ref[...]).astype(BF16)

    o_ref[...] = _dot(xn_ref[...], w_ref[...])


def _inproj(x, g, w, *, tm, tn):
    n, d = x.shape
    c = w.shape[1]
    return pl.pallas_call(
        _inproj_kernel,
        out_shape=jax.ShapeDtypeStruct((n, c), F32),
        grid=(n // tm, c // tn),
        in_specs=[
            pl.BlockSpec((tm, d), lambda i, j: (i, 0)),
            pl.BlockSpec((1, d), lambda i, j: (0, 0)),
            pl.BlockSpec((d, tn), lambda i, j: (0, j)),
        ],
        out_specs=pl.BlockSpec((tm, tn), lambda i, j: (i, j)),
        scratch_shapes=[pltpu.VMEM((tm, d), BF16)],
        compiler_params=_cparams(("parallel", "arbitrary")),
        name="in_proj",
    )(x, g, w)


def _ret_kernel(x_ref, cos_ref, sin_ref, decay_ref, qdec_ref, kdec_ref, s0_ref,
                o_ref, sout_ref, s_ref, *, nt, cdec):
    t = pl.program_id(1)
    hw = RET_HEADS * RET_DK // 2
    per = RET_DK // 2

    @pl.when(t == 0)
    def _():
        s_ref[...] = jnp.zeros(s_ref.shape, F32)
        for h in range(RET_HEADS):
            s_ref[h, h * per:(h + 1) * per, :] = s0_ref[0, h, 0:per, :]
            s_ref[h, hw + h * per:hw + (h + 1) * per, :] = s0_ref[0, h, per:2 * per, :]

    x = x_ref[...]
    c = x.shape[0]
    cos = cos_ref[...]
    sin = sin_ref[...]
    q1, q2 = x[:, 0:hw], x[:, hw:2 * hw]
    k1, k2 = x[:, 2 * hw:3 * hw], x[:, 3 * hw:4 * hw]
    qr = jnp.concatenate([q1 * cos - q2 * sin, q1 * sin + q2 * cos], axis=1)
    kr = jnp.concatenate([k1 * cos - k2 * sin, k1 * sin + k2 * cos], axis=1) * (RET_DK ** -0.5)
    v = x[:, 4 * hw:4 * hw + RET_W]
    g = x[:, 4 * hw + RET_W:4 * hw + 2 * RET_W]
    if c < KEY_PAD:
        kr = jnp.concatenate([kr, jnp.zeros((KEY_PAD - c, kr.shape[1]), F32)], axis=0)
        v = jnp.concatenate([v, jnp.zeros((KEY_PAD - c, v.shape[1]), F32)], axis=0)
    kt = kr.T
    ktb = kt.astype(BF16)
    lane = lax.broadcasted_iota(jnp.int32, (1, 2 * hw), 1)
    head_of_lane = (lane % hw) // per
    for h in range(RET_HEADS):
        qm = jnp.where(head_of_lane == h, qr, 0.0).astype(BF16)
        vh = v[:, h * RET_DV:(h + 1) * RET_DV].astype(BF16)
        sh = s_ref[h]
        att = _dot(qm, ktb) * decay_ref[h]
        o = _dot(att.astype(BF16), vh) + _dot(qm, sh.astype(BF16)) * qdec_ref[h]
        s_ref[h] = sh * cdec[h] + _dot((kt * kdec_ref[h]).astype(BF16), vh)
        o = o * lax.rsqrt(jnp.mean(o * o, axis=-1, keepdims=True) + NORM_EPS)
        gh = g[:, h * RET_DV:(h + 1) * RET_DV]
        o_ref[:, h * RET_DV:(h + 1) * RET_DV] = o * _silu(gh)

    @pl.when(t == nt - 1)
    def _():
        for h in range(RET_HEADS):
            sout_ref[0, h, 0:per, :] = s_ref[h, h * per:(h + 1) * per, :]
            sout_ref[0, h, per:2 * per, :] = s_ref[h, hw + h * per:hw + (h + 1) * per, :]


def _retention(proj, cos, sin, state, *, b, t):
    c = math.gcd(t, RET_CHUNK)
    nt = t // c
    ck = max(c, KEY_PAD)
    hw = RET_HEADS * RET_DK // 2
    hs = np.arange(RET_HEADS, dtype=np.float64)
    log_gamma = np.log1p(-np.exp2(-5.0 - hs))
    j = np.arange(c, dtype=np.float64)
    rel = j[:, None] - j[None, :]
    decay = np.where(rel >= 0, np.exp(log_gamma[:, None, None] * np.where(rel >= 0, rel, 0.0)), 0.0)
    decay_p = np.zeros((RET_HEADS, c, ck))
    decay_p[:, :, :c] = decay
    qdec = np.repeat(np.exp(log_gamma[:, None] * (j + 1.0))[:, :, None], RET_DV, axis=2)
    kdec = np.zeros((RET_HEADS, 1, ck))
    kdec[:, 0, :c] = np.exp(log_gamma[:, None] * (c - 1.0 - j))
    cdec = tuple(float(np.float32(np.exp(lg * c))) for lg in log_gamma)
    kern = functools.partial(_ret_kernel, nt=nt, cdec=cdec)
    return pl.pallas_call(
        kern,
        out_shape=(jax.ShapeDtypeStruct((b * t, RET_W), F32),
                   jax.ShapeDtypeStruct((b, RET_HEADS, RET_DK, RET_DV), F32)),
        grid=(b, nt),
        in_specs=[
            pl.BlockSpec((c, RET_BLOCK_W), lambda i, k: (i * nt + k, 4)),
            pl.BlockSpec((c, hw), lambda i, k: (k, 0)),
            pl.BlockSpec((c, hw), lambda i, k: (k, 0)),
            pl.BlockSpec((RET_HEADS, c, ck), lambda i, k: (0, 0, 0)),
            pl.BlockSpec((RET_HEADS, c, RET_DV), lambda i, k: (0, 0, 0)),
            pl.BlockSpec((RET_HEADS, 1, ck), lambda i, k: (0, 0, 0)),
            pl.BlockSpec((1, RET_HEADS, RET_DK, RET_DV), lambda i, k: (i, 0, 0, 0)),
        ],
        out_specs=(pl.BlockSpec((c, RET_W), lambda i, k: (i * nt + k, 0)),
                   pl.BlockSpec((1, RET_HEADS, RET_DK, RET_DV), lambda i, k: (i, 0, 0, 0))),
        scratch_shapes=[pltpu.VMEM((RET_HEADS, RET_HEADS * RET_DK, RET_DV), F32)],
        compiler_params=_cparams(("parallel", "arbitrary")),
        name="retention",
    )(proj, cos, sin, jnp.asarray(decay_p, F32), jnp.asarray(qdec, F32), jnp.asarray(kdec, F32), state)


def _conv_kernel(x_ref, w_ref, buf_ref, o_ref, bout_ref, carry_ref, *, nt):
    t = pl.program_id(1)
    w = CONV_WIDTH

    @pl.when(t == 0)
    def _():
        carry_ref[...] = jnp.zeros(carry_ref.shape, F32)
        carry_ref[SUBLANES - 2:SUBLANES, :] = buf_ref[0]

    x = x_ref[...]
    rows = x.shape[0]
    gate_b, gate_c, xin = x[:, 0:w], x[:, w:2 * w], x[:, 2 * w:3 * w]
    u = gate_c * xin
    row = lax.broadcasted_iota(jnp.int32, (rows, 1), 0)
    p1 = carry_ref[SUBLANES - 1:SUBLANES, :]
    p2 = carry_ref[SUBLANES - 2:SUBLANES - 1, :]
    u1 = jnp.where(row == 0, p1, pltpu.roll(u, 1, 0))
    u2 = jnp.where(row == 0, p2, jnp.where(row == 1, p1, pltpu.roll(u, 2, 0)))
    y = u2 * w_ref[0:1, :] + u1 * w_ref[1:2, :] + u * w_ref[2:3, :]
    o_ref[...] = gate_b * y
    carry_ref[...] = u[rows - SUBLANES:rows, :]

    @pl.when(t == nt - 1)
    def _():
        bout_ref[0] = u[rows - 2:rows, :]


def _short_conv(proj, w, buf, *, b, t, rows):
    nt = t // rows
    kern = functools.partial(_conv_kernel, nt=nt)
    return pl.pallas_call(
        kern,
        out_shape=(jax.ShapeDtypeStruct((b * t, CONV_WIDTH), F32),
                   jax.ShapeDtypeStruct((b, CONV_K - 1, CONV_WIDTH), F32)),
        grid=(b, nt),
        in_specs=[
            pl.BlockSpec((rows, CONV_BLOCK_W), lambda i, k: (i * nt + k, 5)),
            pl.BlockSpec((CONV_K, CONV_WIDTH), lambda i, k: (0, 0)),
            pl.BlockSpec((1, CONV_K - 1, CONV_WIDTH), lambda i, k: (i, 0, 0)),
        ],
        out_specs=(pl.BlockSpec((rows, CONV_WIDTH), lambda i, k: (i * nt + k, 0)),
                   pl.BlockSpec((1, CONV_K - 1, CONV_WIDTH), lambda i, k: (i, 0, 0))),
        scratch_shapes=[pltpu.VMEM((SUBLANES, CONV_WIDTH), F32)],
        compiler_params=_cparams(("parallel", "arbitrary")),
        name="short_conv",
    )(proj, w, buf)


def _split3(x):
    hi = x.astype(BF16)
    r = x - hi.astype(F32)
    mid = r.astype(BF16)
    lo = (r - mid.astype(F32)).astype(BF16)
    return hi, mid, lo


def _hgrn_intra_exact(gcum, qh, kk, hv, g_ref, k_ref, oi_ref):
    ell = gcum.shape[0]
    dk = HGRN_DK
    g_ref[...] = gcum
    k_ref[...] = kk
    nsub = ell // HGRN_SUB
    lane = lax.broadcasted_iota(jnp.int32, (1, KEY_PAD), 1)
    key_row = lax.broadcasted_iota(jnp.int32, (ell, 1), 0)
    sub_row = lax.broadcasted_iota(jnp.int32, (HGRN_SUB, 1), 0)
    a_blocks = [[None] * nsub for _ in range(HGRN_HEADS)]
    for j in range(nsub):
        r0 = j * HGRN_SUB
        gq = gcum[r0:r0 + HGRN_SUB, :]
        qq = qh[r0:r0 + HGRN_SUB, :]

        def diag_body(s, carry, r0=r0, gq=gq, qq=qq):
            gs = g_ref[pl.ds(r0 + s, 1), :]
            ks = k_ref[pl.ds(r0 + s, 1), :]
            valid = sub_row >= s
            p = jnp.where(valid, jnp.exp(jnp.minimum(gq - gs, 0.0)) * qq * ks, 0.0)
            out = []
            for h in range(HGRN_HEADS):
                a = jnp.sum(p[:, h * dk:(h + 1) * dk], axis=-1, keepdims=True)
                out.append(carry[h] + jnp.where(lane == r0 + s, a, 0.0))
            return tuple(out)

        init = tuple(jnp.zeros((HGRN_SUB, KEY_PAD), F32) for _ in range(HGRN_HEADS))
        diag = lax.fori_loop(0, HGRN_SUB, diag_body, init)
        if j > 0:
            gb = gcum[r0 - 1:r0, :]
            q_t = (qq * jnp.exp(gq - gb)).astype(BF16)
            k_t = jnp.where(key_row < r0, kk * jnp.exp(jnp.minimum(gb - gcum, 0.0)), 0.0)
            if ell < KEY_PAD:
                k_t = jnp.concatenate([k_t, jnp.zeros((KEY_PAD - ell, k_t.shape[1]), F32)], axis=0)
            k_t = k_t.astype(BF16)
        for h in range(HGRN_HEADS):
            blk = diag[h]
            if j > 0:
                blk = blk + _dot_nt(q_t[:, h * dk:(h + 1) * dk], k_t[:, h * dk:(h + 1) * dk])
            a_blocks[h][j] = blk
    for h in range(HGRN_HEADS):
        a_h = a_blocks[h][0] if nsub == 1 else jnp.concatenate(a_blocks[h], axis=0)
        vh = hv[:, h * HGRN_DV:(h + 1) * HGRN_DV]
        if ell < KEY_PAD:
            vh = jnp.concatenate([vh, jnp.zeros((KEY_PAD - ell, HGRN_DV), F32)], axis=0)
        oi_ref[h] = _dot(a_h.astype(BF16), vh.astype(BF16))


def _hgrn_intra_fast(gcum, g_end, qh, kk, hv, oi_ref):
    ell = gcum.shape[0]
    dk = HGRN_DK
    gm = 0.5 * g_end
    q_t = (qh * jnp.exp(gcum - gm)).astype(BF16)
    k_t = (kk * jnp.exp(gm - gcum)).astype(BF16)
    causal = lax.broadcasted_iota(jnp.int32, (ell, ell), 0) >= lax.broadcasted_iota(jnp.int32, (ell, ell), 1)
    for h in range(HGRN_HEADS):
        a = jnp.where(causal, _dot_nt(q_t[:, h * dk:(h + 1) * dk], k_t[:, h * dk:(h + 1) * dk]), 0.0)
        oi_ref[h] = _dot(a.astype(BF16), hv[:, h * HGRN_DV:(h + 1) * HGRN_DV].astype(BF16))


def _hgrn_kernel(x_ref, lb_ref, gn_ref, s0_ref, o_ref, sout_ref,
                 st_ref, g_ref, k_ref, oi_ref, *, nt, chunk):
    t = pl.program_id(1)
    w = HGRN_W
    dk = HGRN_DK

    @pl.when(t == 0)
    def _():
        for h in range(HGRN_HEADS):
            st_ref[h] = s0_ref[0, h].T

    rows = x_ref.shape[0]
    lb = lb_ref[...]
    gn = gn_ref[...]
    ri = lax.broadcasted_iota(jnp.int32, (chunk, chunk), 0)
    ci = lax.broadcasted_iota(jnp.int32, (chunk, chunk), 1)
    tri = jnp.where(ri >= ci, 1.0, 0.0).astype(BF16)
    for c in range(max(rows // chunk, 1)):
        if rows >= chunk:
            x = x_ref[c * chunk:(c + 1) * chunk, :]
            live = chunk
        else:
            x = jnp.concatenate([x_ref[...], jnp.zeros((chunk - rows, x_ref.shape[1]), F32)], axis=0)
            live = rows
        hq, hf, hv, hg = x[:, 0:w], x[:, w:2 * w], x[:, 2 * w:3 * w], x[:, 3 * w:4 * w]
        f_gate = lb + (1.0 - lb) * jax.nn.sigmoid(hf)
        log_f = jnp.log(f_gate)
        kk = 1.0 - f_gate
        if live < chunk:
            ok = lax.broadcasted_iota(jnp.int32, (chunk, 1), 0) < live
            log_f = jnp.where(ok, log_f, 0.0)
            kk = jnp.where(ok, kk, 0.0)
        qh = _silu(hq)
        p_hi, p_mid, p_lo = _split3(log_f)
        gcum = _dot(tri, p_hi) + _dot(tri, p_mid) + _dot(tri, p_lo)
        g_end = gcum[chunk - 1:chunk, :]

        safe = jnp.min(g_end) > -HGRN_SAFE_LOG_DECAY

        @pl.when(safe)
        def _():
            _hgrn_intra_fast(gcum, g_end, qh, kk, hv, oi_ref)

        @pl.when(jnp.logical_not(safe))
        def _():
            _hgrn_intra_exact(gcum, qh, kk, hv, g_ref, k_ref, oi_ref)

        q_in = (qh * jnp.exp(gcum)).astype(BF16)
        k_out = (kk * jnp.exp(g_end - gcum)).astype(BF16)
        e_end = jnp.exp(g_end)
        for h in range(HGRN_HEADS):
            sl = slice(h * dk, (h + 1) * dk)
            st = st_ref[h]
            o = oi_ref[h] + _dot_nt(q_in[:, sl], st.astype(BF16))
            st_ref[h] = st * e_end[:, sl] + _dot_tn(hv[:, h * HGRN_DV:(h + 1) * HGRN_DV].astype(BF16), k_out[:, sl])
            o = _rms(o, gn) * _silu(hg[:, h * HGRN_DV:(h + 1) * HGRN_DV])
            o_ref[c * chunk:c * chunk + live, h * HGRN_DV:(h + 1) * HGRN_DV] = o[0:live, :]

    @pl.when(t == nt - 1)
    def _():
        for h in range(HGRN_HEADS):
            sout_ref[0, h] = st_ref[h].T


def _hgrn(proj, lb, gnorm, state, *, b, t, rows):
    nt = t // rows
    chunk = max(min(rows, HGRN_CHUNK), HGRN_SUB)
    kern = functools.partial(_hgrn_kernel, nt=nt, chunk=chunk)
    return pl.pallas_call(
        kern,
        out_shape=(jax.ShapeDtypeStruct((b * t, HGRN_W), F32),
                   jax.ShapeDtypeStruct((b, HGRN_HEADS, HGRN_DK, HGRN_DV), F32)),
        grid=(b, nt),
        in_specs=[
            pl.BlockSpec((rows, HG_BLOCK_W), lambda i, k: (i * nt + k, 2)),
            pl.BlockSpec((1, HGRN_W), lambda i, k: (0, 0)),
            pl.BlockSpec((1, HGRN_DV), lambda i, k: (0, 0)),
            pl.BlockSpec((1, HGRN_HEADS, HGRN_DK, HGRN_DV), lambda i, k: (i, 0, 0, 0)),
        ],
        out_specs=(pl.BlockSpec((rows, HGRN_W), lambda i, k: (i * nt + k, 0)),
                   pl.BlockSpec((1, HGRN_HEADS, HGRN_DK, HGRN_DV), lambda i, k: (i, 0, 0, 0))),
        scratch_shapes=[pltpu.VMEM((HGRN_HEADS, HGRN_DV, HGRN_DK), F32),
                        pltpu.VMEM((chunk, HGRN_W), F32),
                        pltpu.VMEM((chunk, HGRN_W), F32),
                        pltpu.VMEM((HGRN_HEADS, chunk, HGRN_DV), F32)],
        compiler_params=_cparams(("parallel", "arbitrary")),
        name="hgrn2",
    )(proj, lb, gnorm, state)


def _mla_prep_kernel(x_ref, qn_ref, kvn_ref, wqn_ref, wqp_ref, wqr_ref, eq_ref,
                     cq_ref, sq_ref, ck_ref, sk_ref, *rest, with_kv):
    if with_kv:
        wuk_ref, ek_ref, wuv_ref, q_ref, lat_ref, kpe_ref, kc_ref, v_ref = rest
    else:
        q_ref, lat_ref, kpe_ref = rest
    x = x_ref[...]
    mcq = x[:, 0:MLA_Q_LORA]
    mckv = x[:, MLA_Q_LORA:MLA_Q_LORA + MLA_KV_LORA]
    o = MLA_Q_LORA + MLA_KV_LORA
    mkr = x[:, o:o + MLA_ROPE]
    mkr_rot = x[:, o + MLA_ROPE:o + 2 * MLA_ROPE]
    xq = _rms(mcq, qn_ref[...]).astype(BF16)
    q_pe = _dot(xq, wqp_ref[...]) * cq_ref[...] + _dot(xq, wqr_ref[...]) * sq_ref[...]
    q_cat = _dot(xq, wqn_ref[...]) + _dot(q_pe.astype(BF16), eq_ref[...])
    q_ref[...] = (q_cat * (MLA_SCALE * LOG2E)).astype(q_ref.dtype)
    ckv = _rms(mckv, kvn_ref[...])
    lat_ref[...] = ckv
    kpe = mkr * ck_ref[...] + mkr_rot * sk_ref[...]
    kpe_ref[...] = kpe
    if with_kv:
        cb = ckv.astype(BF16)
        kc_ref[...] = (_dot(cb, wuk_ref[...]) + _dot(kpe.astype(BF16), ek_ref[...])).astype(BF16)
        v_ref[0] = _dot_nt(wuv_ref[...], cb).astype(BF16)


def _mla_prep(proj, lw, tabs, *, n, t, tm, with_kv):
    rows = min(tm, t)
    nt = t // rows
    hp = MLA_HEADS * MLA_HEAD_PAD
    qp_w = MLA_HEADS * MLA_ROPE
    full = lambda r, c: pl.BlockSpec((r, c), lambda i: (0, 0))
    tab = lambda c: pl.BlockSpec((rows, c), lambda i: (i % nt, 0))
    in_specs = [
        pl.BlockSpec((rows, MLA_BLOCK_W), lambda i: (i, 12)),
        full(1, MLA_Q_LORA), full(1, MLA_KV_LORA),
        full(MLA_Q_LORA, hp), full(MLA_Q_LORA, qp_w), full(MLA_Q_LORA, qp_w), full(qp_w, hp),
        tab(qp_w), tab(qp_w), tab(MLA_ROPE), tab(MLA_ROPE),
    ]
    args = [proj, lw["q_norm"], lw["kv_norm"], lw["wq_nope"], lw["wq_pe"], lw["wq_rot"], lw["e_q"],
            tabs["cos_q"], tabs["sin_q"], tabs["cos_k"], tabs["sin_k"]]
    q_dtype = BF16 if with_kv else F32
    out_shape = [jax.ShapeDtypeStruct((n, hp), q_dtype),
                 jax.ShapeDtypeStruct((n, MLA_KV_LORA), F32),
                 jax.ShapeDtypeStruct((n, MLA_ROPE), F32)]
    out_specs = [pl.BlockSpec((rows, hp), lambda i: (i, 0)),
                 pl.BlockSpec((rows, MLA_KV_LORA), lambda i: (i, 0)),
                 pl.BlockSpec((rows, MLA_ROPE), lambda i: (i, 0))]
    if with_kv:
        in_specs += [full(MLA_KV_LORA, hp), full(MLA_ROPE, hp), full(MLA_W, MLA_KV_LORA)]
        args += [lw["wuk_pad"], lw["e_k"], lw["wuv_t"]]
        out_shape += [jax.ShapeDtypeStruct((n, hp), BF16), jax.ShapeDtypeStruct((n // t, MLA_W, t), BF16)]
        out_specs += [pl.BlockSpec((rows, hp), lambda i: (i, 0)),
                      pl.BlockSpec((1, MLA_W, rows), lambda i: (i // nt, 0, i % nt))]
    return pl.pallas_call(
        functools.partial(_mla_prep_kernel, with_kv=with_kv),
        out_shape=tuple(out_shape),
        grid=(n // rows,),
        in_specs=in_specs,
        out_specs=tuple(out_specs),
        compiler_params=_cparams(("parallel",)),
        name="mla_prep",
    )(*args)


def _flash_kernel(q_ref, k_ref, vt_ref, o_ref, m_ref, l_ref, acc_ref, *, tq, tk):
    qi = pl.program_id(1)
    kj = pl.program_id(2)
    hp = MLA_HEAD_PAD

    @pl.when(kj == 0)
    def _():
        m_ref[...] = jnp.full(m_ref.shape, NEG_BIG, F32)
        l_ref[...] = jnp.zeros(l_ref.shape, F32)
        acc_ref[...] = jnp.zeros(acc_ref.shape, F32)

    def step(mask):
        for h in range(MLA_HEADS):
            q = q_ref[:, h * hp:(h + 1) * hp]
            k = k_ref[:, h * hp:(h + 1) * hp]
            s = _dot_nt(k, q)
            if mask is not None:
                s = jnp.where(mask, s, NEG_BIG)
            m_old = m_ref[h]
            m_new = jnp.maximum(m_old, jnp.max(s, axis=0, keepdims=True))
            alpha = jnp.exp2(m_old - m_new)
            p = jnp.exp2(s - m_new)
            l_ref[h] = l_ref[h] * alpha + jnp.sum(p, axis=0, keepdims=True)
            vt = vt_ref[0, h * MLA_V:(h + 1) * MLA_V, :]
            acc_ref[h] = acc_ref[h] * alpha + _dot(vt, p.astype(BF16))
            m_ref[h] = m_new

    @pl.when(kj < qi)
    def _():
        step(None)

    @pl.when(kj == qi)
    def _():
        step(lax.broadcasted_iota(jnp.int32, (tk, tq), 0) <= lax.broadcasted_iota(jnp.int32, (tk, tq), 1))
        for h in range(MLA_HEADS):
            acc_ref[h] = acc_ref[h] / l_ref[h]
        o_ref[...] = acc_ref[...].reshape(MLA_W, tq).T


def _flash(q, k, vt, *, b, t, tq, tk):
    nq, nk = t // tq, t // tk
    hp = MLA_HEADS * MLA_HEAD_PAD
    kern = functools.partial(_flash_kernel, tq=tq, tk=tk)
    return pl.pallas_call(
        kern,
        out_shape=jax.ShapeDtypeStruct((b * t, MLA_W), F32),
        grid=(b, nq, nk),
        in_specs=[
            pl.BlockSpec((tq, hp), lambda i, a, c: (i * nq + a, 0)),
            pl.BlockSpec((tk, hp), lambda i, a, c: (i * nk + jnp.minimum(c, a), 0)),
            pl.BlockSpec((1, MLA_W, tk), lambda i, a, c: (i, 0, jnp.minimum(c, a))),
        ],
        out_specs=pl.BlockSpec((tq, MLA_W), lambda i, a, c: (i * nq + a, 0)),
        scratch_shapes=[pltpu.VMEM((MLA_HEADS, 1, tq), F32),
                        pltpu.VMEM((MLA_HEADS, 1, tq), F32),
                        pltpu.VMEM((MLA_HEADS, MLA_V, tq), F32)],
        compiler_params=_cparams(("parallel", "parallel", "arbitrary")),
        name="mla_flash",
    )(q, k, vt)


def _paged_kernel(pt_ref, q_ref, lat_new_ref, kpe_new_ref, wabs_ref, wuv_ref, *rest,
                  ng, pg, t):
    lat_refs = rest[0:pg]
    pe_refs = rest[pg:2 * pg]
    o_ref = rest[2 * pg]
    ql_ref, qp_ref, m_ref, l_ref, acc_ref, c_ref, rt_ref = rest[2 * pg + 1:]
    g = pl.program_id(1)
    hp = MLA_HEAD_PAD
    rows = MLA_HEADS * t

    @pl.when(g == 0)
    def _():
        q = q_ref[0]
        lane_head = lax.broadcasted_iota(jnp.int32, (1, MLA_HEADS * hp), 1) // hp
        q_exp = jnp.concatenate(
            [jnp.where(lane_head == h, q, 0.0) for h in range(MLA_HEADS)], axis=0).astype(BF16)
        q_abs = _dot(q_exp, wabs_ref[...])
        ql_ref[...] = q_abs[:, 0:MLA_KV_LORA].astype(BF16)
        qp_ref[...] = q_abs[:, MLA_KV_LORA:MLA_KV_LORA + LANES].astype(BF16)
        m_ref[...] = jnp.full(m_ref.shape, NEG_BIG, F32)
        l_ref[...] = jnp.zeros(l_ref.shape, F32)
        acc_ref[...] = jnp.zeros(acc_ref.shape, F32)

    ql = ql_ref[...]
    qp = qp_ref[:, 0:MLA_ROPE]

    def update(s, c):
        m_old = m_ref[...]
        m_new = jnp.maximum(m_old, jnp.max(s, axis=-1, keepdims=True))
        alpha = jnp.exp2(m_old - m_new)
        p = jnp.exp2(s - m_new)
        l_ref[...] = l_ref[...] * alpha + jnp.sum(p, axis=-1, keepdims=True)
        acc_ref[...] = acc_ref[...] * alpha + _dot(p.astype(BF16), c)
        m_ref[...] = m_new

    for i in range(pg):
        c_ref[i * PAGE_SIZE:(i + 1) * PAGE_SIZE, :] = lat_refs[i][0, 0].astype(BF16)
        rt_ref[:, i * PAGE_SIZE:(i + 1) * PAGE_SIZE] = pe_refs[i][0, 0].astype(BF16)
    c_all = c_ref[...]
    update(_dot_nt(ql, c_all) + _dot(qp, rt_ref[...]), c_all)

    @pl.when(g == ng - 1)
    def _():
        cn = jnp.concatenate([lat_new_ref[0], jnp.zeros((KEY_PAD - t, MLA_KV_LORA), F32)], axis=0).astype(BF16)
        rn = jnp.concatenate([kpe_new_ref[0], jnp.zeros((KEY_PAD - t, MLA_ROPE), F32)], axis=0).astype(BF16)
        s = _dot_nt(ql, cn) + _dot_nt(qp, rn)
        qtok = lax.broadcasted_iota(jnp.int32, (rows, KEY_PAD), 0) % t
        kidx = lax.broadcasted_iota(jnp.int32, (rows, KEY_PAD), 1)
        s = jnp.where(kidx <= qtok, s, NEG_BIG)
        update(s, cn)
        o_lat = acc_ref[...] / l_ref[...]
        res = _dot(o_lat.astype(BF16), wuv_ref[...])
        lane_head = lax.broadcasted_iota(jnp.int32, (1, MLA_W), 1) // MLA_V
        out = jnp.zeros((t, MLA_W), F32)
        for h in range(MLA_HEADS):
            out = out + jnp.where(lane_head == h, res[h * t:(h + 1) * t, :], 0.0)
        o_ref[0] = out


def _paged_attention(page_table, q, lat_new, kpe_new, wabs, wuv, lat_pool, pe_pool, *, layer, pg):
    b, t, hp = q.shape
    n_pages = page_table.shape[1]
    ng = n_pages // pg
    rows = MLA_HEADS * t

    def page_spec(r, c, i):
        return pl.BlockSpec((1, 1, r, c), lambda bi, g, pt: (layer, pt[bi, g * pg + i], 0, 0))

    in_specs = [
        pl.BlockSpec((1, t, hp), lambda bi, g, pt: (bi, 0, 0)),
        pl.BlockSpec((1, t, MLA_KV_LORA), lambda bi, g, pt: (bi, 0, 0)),
        pl.BlockSpec((1, t, MLA_ROPE), lambda bi, g, pt: (bi, 0, 0)),
        pl.BlockSpec(wabs.shape, lambda bi, g, pt: (0, 0)),
        pl.BlockSpec(wuv.shape, lambda bi, g, pt: (0, 0)),
    ]
    in_specs += [page_spec(PAGE_SIZE, MLA_KV_LORA, i) for i in range(pg)]
    in_specs += [page_spec(MLA_ROPE, PAGE_SIZE, i) for i in range(pg)]
    kern = functools.partial(_paged_kernel, ng=ng, pg=pg, t=t)
    return pl.pallas_call(
        kern,
        out_shape=jax.ShapeDtypeStruct((b, t, MLA_W), F32),
        grid_spec=pltpu.PrefetchScalarGridSpec(
            num_scalar_prefetch=1,
            grid=(b, ng),
            in_specs=in_specs,
            out_specs=pl.BlockSpec((1, t, MLA_W), lambda bi, g, pt: (bi, 0, 0)),
            scratch_shapes=[pltpu.VMEM((rows, MLA_KV_LORA), BF16),
                            pltpu.VMEM((rows, LANES), BF16),
                            pltpu.VMEM((rows, 1), F32),
                            pltpu.VMEM((rows, 1), F32),
                            pltpu.VMEM((rows, MLA_KV_LORA), F32),
                            pltpu.VMEM((pg * PAGE_SIZE, MLA_KV_LORA), BF16),
                            pltpu.VMEM((MLA_ROPE, pg * PAGE_SIZE), BF16)],
        ),
        compiler_params=_cparams(("parallel", "arbitrary")),
        name="mla_paged",
    )(page_table, q, lat_new, kpe_new, wabs, wuv, *([lat_pool] * pg), *([pe_pool] * pg))


def _merge_kernel(h_ref, gate_ref, b0_ref, b1_ref, b2_ref, b3_ref, wbo_ref, wout_ref, o_ref):
    branches = (b0_ref, b1_ref, b2_ref, b3_ref)
    merged = None
    off = 0
    for i, br in enumerate(branches):
        wd = br.shape[1]
        proj = _dot(br[...].astype(BF16), wbo_ref[off:off + wd, :])
        term = jax.nn.sigmoid(gate_ref[:, i * D_MODEL:(i + 1) * D_MODEL]) * proj
        merged = term if merged is None else merged + term
        off += wd
    o_ref[...] = h_ref[...] + _dot(merged.astype(BF16), wout_ref[...])


def _merge(h, proj, branches, wbo, wout, *, tm):
    n = h.shape[0]
    row = lambda c: pl.BlockSpec((tm, c), lambda i: (i, 0))
    return pl.pallas_call(
        _merge_kernel,
        out_shape=jax.ShapeDtypeStruct((n, D_MODEL), F32),
        grid=(n // tm,),
        in_specs=[row(D_MODEL), row(GATE_W)] + [row(br.shape[1]) for br in branches]
        + [pl.BlockSpec(wbo.shape, lambda i: (0, 0)), pl.BlockSpec(wout.shape, lambda i: (0, 0))],
        out_specs=row(D_MODEL),
        compiler_params=_cparams(("parallel",)),
        name="branch_merge",
    )(h, proj, *branches, wbo, wout)


def _ffn_kernel(h_ref, g_ref, w1_ref, w2_ref, gf_ref, o_ref, xn_ref, acc_ref, *, nf, final_norm):
    j = pl.program_id(1)

    @pl.when(j == 0)
    def _():
        xn_ref[...] = _rms(h_ref[...], g_ref[...]).astype(BF16)
        acc_ref[...] = jnp.zeros(acc_ref.shape, F32)

    u = _dot(xn_ref[...], w1_ref[...])
    u = jnp.square(jnp.maximum(u, 0.0))
    acc_ref[...] += _dot(u.astype(BF16), w2_ref[...])

    @pl.when(j == nf - 1)
    def _():
        out = h_ref[...] + acc_ref[...]
        if final_norm:
            out = _rms(out, gf_ref[...])
        o_ref[...] = out


def _ffn(h, g, w1, w2, gf, *, tm, tf, final_norm):
    n, d = h.shape
    f = w1.shape[1]
    nf = f // tf
    kern = functools.partial(_ffn_kernel, nf=nf, final_norm=final_norm)
    return pl.pallas_call(
        kern,
        out_shape=jax.ShapeDtypeStruct((n, d), F32),
        grid=(n // tm, nf),
        in_specs=[
            pl.BlockSpec((tm, d), lambda i, j: (i, 0)),
            pl.BlockSpec((1, d), lambda i, j: (0, 0)),
            pl.BlockSpec((d, tf), lambda i, j: (0, j)),
            pl.BlockSpec((tf, d), lambda i, j: (j, 0)),
            pl.BlockSpec((1, d), lambda i, j: (0, 0)),
        ],
        out_specs=pl.BlockSpec((tm, d), lambda i, j: (i, 0)),
        scratch_shapes=[pltpu.VMEM((tm, d), BF16), pltpu.VMEM((tm, d), F32)],
        compiler_params=_cparams(("parallel", "arbitrary")),
        name="relu2_mlp",
    )(h, g, w1, w2, gf)


def _rot_partner(w, half):
    return jnp.concatenate([-w[..., half:], w[..., :half]], axis=-1)


def _pack_layer(w_in_l, conv_w_l, q_norm_l, w_uq_l, kv_norm_l, w_uk_l, w_uv_l, hgrn_norm_l,
                w_bo_l, w_out_l, g_mix_l, g_ffn_l, w_ff1_l, w_ff2_l):
    d = w_in_l.shape[0]
    offs = np.cumsum((0,) + IN_SIZES)
    col = lambda i: w_in_l[:, int(offs[i]):int(offs[i + 1])]
    rq, rk, rv, rg, cb, cc, cx, mcq, mckv, mkr, hq, hf, hi, hg, mg = [col(i) for i in range(15)]

    def halves(w):
        w4 = w.reshape(d, RET_HEADS, 2, RET_DK // 2)
        return w4[:, :, 0, :].reshape(d, -1), w4[:, :, 1, :].reshape(d, -1)

    rq1, rq2 = halves(rq)
    rk1, rk2 = halves(rk)
    mkr_rot = _rot_partner(mkr, MLA_ROPE // 2)
    pad = jnp.zeros((d, MLA_BLOCK_W - MLA_Q_LORA - MLA_KV_LORA - 2 * MLA_ROPE), w_in_l.dtype)
    packed = jnp.concatenate([mg, hq, hf, hi, hg, rq1, rq2, rk1, rk2, rv, rg, cb, cc, cx,
                              mcq, mckv, mkr, mkr_rot, pad], axis=1).astype(BF16)

    qd = MLA_NOPE + MLA_ROPE
    wq = w_uq_l.reshape(MLA_Q_LORA, MLA_HEADS, qd)
    wq_nope = jnp.concatenate(
        [wq[:, :, :MLA_NOPE], jnp.zeros((MLA_Q_LORA, MLA_HEADS, MLA_HEAD_PAD - MLA_NOPE), wq.dtype)],
        axis=2).reshape(MLA_Q_LORA, MLA_HEADS * MLA_HEAD_PAD)
    wq_pe = wq[:, :, MLA_NOPE:]
    wq_rot = _rot_partner(wq_pe, MLA_ROPE // 2)
    e_q = np.zeros((MLA_HEADS * MLA_ROPE, MLA_HEADS * MLA_HEAD_PAD), np.float32)
    e_k = np.zeros((MLA_ROPE, MLA_HEADS * MLA_HEAD_PAD), np.float32)
    for h in range(MLA_HEADS):
        for i in range(MLA_ROPE):
            e_q[h * MLA_ROPE + i, h * MLA_HEAD_PAD + MLA_NOPE + i] = 1.0
            e_k[i, h * MLA_HEAD_PAD + MLA_NOPE + i] = 1.0
    wuk_pad = jnp.concatenate(
        [w_uk_l, jnp.zeros((MLA_KV_LORA, MLA_HEADS, MLA_HEAD_PAD - MLA_NOPE), w_uk_l.dtype)],
        axis=2).reshape(MLA_KV_LORA, MLA_HEADS * MLA_HEAD_PAD)
    wabs = jnp.zeros((MLA_HEADS, MLA_HEAD_PAD, MLA_KV_LORA + LANES), F32)
    wabs = wabs.at[:, :MLA_NOPE, :MLA_KV_LORA].set(jnp.transpose(w_uk_l, (1, 2, 0)))
    wabs = wabs.at[:, MLA_NOPE:MLA_NOPE + MLA_ROPE, MLA_KV_LORA:MLA_KV_LORA + MLA_ROPE].set(
        jnp.broadcast_to(jnp.eye(MLA_ROPE, dtype=F32), (MLA_HEADS, MLA_ROPE, MLA_ROPE)))
    wabs = wabs.reshape(MLA_HEADS * MLA_HEAD_PAD, MLA_KV_LORA + LANES)
    return {
        "w_in": packed,
        "g_mix": g_mix_l.reshape(1, d),
        "conv_w": conv_w_l,
        "q_norm": q_norm_l.reshape(1, -1),
        "kv_norm": kv_norm_l.reshape(1, -1),
        "wq_nope": wq_nope.astype(BF16),
        "wq_pe": wq_pe.reshape(MLA_Q_LORA, -1).astype(BF16),
        "wq_rot": wq_rot.reshape(MLA_Q_LORA, -1).astype(BF16),
        "e_q": jnp.asarray(e_q, BF16),
        "e_k": jnp.asarray(e_k, BF16),
        "wuk_pad": wuk_pad.astype(BF16),
        "wuv": w_uv_l.reshape(MLA_KV_LORA, MLA_W).astype(BF16),
        "wuv_t": w_uv_l.reshape(MLA_KV_LORA, MLA_W).T.astype(BF16),
        "wabs": wabs.astype(BF16),
        "hgrn_norm": hgrn_norm_l.reshape(1, -1),
        "w_bo": w_bo_l.astype(BF16),
        "w_out": w_out_l.astype(BF16),
        "g_ffn": g_ffn_l.reshape(1, d),
        "w_ff1": w_ff1_l.astype(BF16),
        "w_ff2": w_ff2_l.astype(BF16),
    }


def _rope_tables(pos):
    def cs(half):
        inv = ROPE_THETA ** (-jnp.arange(half, dtype=F32) / half)
        ang = pos.astype(F32)[:, None] * inv[None, :]
        return jnp.cos(ang), jnp.sin(ang)

    c_ret, s_ret = cs(RET_DK // 2)
    c_pe, s_pe = cs(MLA_ROPE // 2)
    two = lambda a: jnp.concatenate([a, a], axis=1)
    return {
        "cos_ret": jnp.tile(c_ret, (1, RET_HEADS)), "sin_ret": jnp.tile(s_ret, (1, RET_HEADS)),
        "cos_q": jnp.tile(two(c_pe), (1, MLA_HEADS)), "sin_q": jnp.tile(two(s_pe), (1, MLA_HEADS)),
        "cos_k": two(c_pe), "sin_k": two(s_pe),
    }


def _tiles(t):
    return {"conv": min(512, t), "hgrn": min(128, t), "prep": min(512, t), "tq": min(512, t), "tk": min(512, t)}


def _trunk_layer(h, tabs, ret_state, conv_buf, hgrn_state, lb, lw, *, b, t, attend, final_gain):
    n = b * t
    tl = _tiles(t)
    big = min(n, 1024)
    proj = _inproj(h, lw["g_mix"], lw["w_in"], tm=big, tn=768)
    o_ret, ret_new = _retention(proj, tabs["cos_ret"], tabs["sin_ret"], ret_state, b=b, t=t)
    o_conv, conv_new = _short_conv(proj, lw["conv_w"], conv_buf, b=b, t=t, rows=tl["conv"])
    o_h, hgrn_new = _hgrn(proj, lb, lw["hgrn_norm"], hgrn_state, b=b, t=t, rows=tl["hgrn"])
    o_mla, ckv, kpe = attend(proj, lw, tabs, tl)
    h = _merge(h, proj, (o_ret, o_conv, o_mla, o_h), lw["w_bo"], lw["w_out"], tm=min(n, 256))
    gf = final_gain if final_gain is not None else lw["g_ffn"]
    h = _ffn(h, lw["g_ffn"], lw["w_ff1"], lw["w_ff2"], gf, tm=big, tf=512,
             final_norm=final_gain is not None)
    return h, ret_new, conv_new, hgrn_new, ckv, kpe


def kernel(x_prompt, x_sample, cache_mla_latent, cache_mla_rope, page_table,
           state_retention, state_conv, state_hgrn,
           norm_mix, w_in, conv_w, mla_q_norm, mla_w_uq, mla_kv_norm, mla_w_uk, mla_w_uv,
           hgrn_lb_logits, hgrn_out_norm, w_branch_out, w_out, norm_ffn, w_ff1, w_ff2, norm_final):
    bp, tp, d = x_prompt.shape
    bs, ts, _ = x_sample.shape
    depth = w_in.shape[0]
    past_len = page_table.shape[1] * PAGE_SIZE
    tabs_p = _rope_tables(jnp.arange(tp, dtype=jnp.int32))
    tabs_s = _rope_tables(past_len + jnp.arange(ts, dtype=jnp.int32))
    lb_p = jax.nn.softmax(hgrn_lb_logits.astype(F32), axis=0)
    lower_bounds = jnp.cumsum(lb_p, axis=0) - lb_p[0:1]
    gfin = norm_final.reshape(1, d)
    rope_pool_t = jnp.swapaxes(cache_mla_rope, 2, 3)

    hp = x_prompt.reshape(bp * tp, d)
    hs = x_sample.reshape(bs * ts, d)
    zeros_ret = jnp.zeros((bp, RET_HEADS, RET_DK, RET_DV), F32)
    zeros_conv = jnp.zeros((bp, CONV_K - 1, CONV_WIDTH), F32)
    zeros_hgrn = jnp.zeros((bp, HGRN_HEADS, HGRN_DK, HGRN_DV), F32)
    outs = {k: [] for k in ("lat_p", "rope_p", "lat_s", "rope_s", "ret_p", "ret_s",
                            "conv_p", "conv_s", "hg_p", "hg_s")}
    for l in range(depth):
        lw = _pack_layer(w_in[l], conv_w[l], mla_q_norm[l], mla_w_uq[l], mla_kv_norm[l], mla_w_uk[l],
                         mla_w_uv[l], hgrn_out_norm[l], w_branch_out[l], w_out[l], norm_mix[l],
                         norm_ffn[l], w_ff1[l], w_ff2[l])
        lb = lower_bounds[l].reshape(1, HGRN_W)
        last = gfin if l == depth - 1 else None

        def attend_prompt(proj, lw, tabs, tl):
            q, ckv, kpe, kc, v = _mla_prep(proj, lw, tabs, n=bp * tp, t=tp, tm=tl["prep"], with_kv=True)
            o = _flash(q, kc, v, b=bp, t=tp, tq=tl["tq"], tk=tl["tk"])
            return o, ckv, kpe

        def attend_sample(proj, lw, tabs, tl, layer=l):
            q, ckv, kpe = _mla_prep(proj, lw, tabs, n=bs * ts, t=ts, tm=tl["prep"], with_kv=False)
            o = _paged_attention(page_table, q.reshape(bs, ts, -1), ckv.reshape(bs, ts, -1),
                                 kpe.reshape(bs, ts, -1), lw["wabs"], lw["wuv"],
                                 cache_mla_latent, rope_pool_t, layer=layer,
                                 pg=math.gcd(16, page_table.shape[1]))
            return o.reshape(bs * ts, MLA_W), ckv, kpe

        hp, r, c, g, ckv, kpe = _trunk_layer(hp, tabs_p, zeros_ret, zeros_conv, zeros_hgrn, lb, lw,
                                             b=bp, t=tp, attend=attend_prompt, final_gain=last)
        outs["ret_p"].append(r); outs["conv_p"].append(c); outs["hg_p"].append(g)
        outs["lat_p"].append(ckv.reshape(bp, tp, -1)); outs["rope_p"].append(kpe.reshape(bp, tp, -1))
        hs, r, c, g, ckv, kpe = _trunk_layer(hs, tabs_s, state_retention[l], state_conv[l], state_hgrn[l],
                                             lb, lw, b=bs, t=ts, attend=attend_sample, final_gain=last)
        outs["ret_s"].append(r); outs["conv_s"].append(c); outs["hg_s"].append(g)
        outs["lat_s"].append(ckv.reshape(bs, ts, -1)); outs["rope_s"].append(kpe.reshape(bs, ts, -1))

    st = lambda k: jnp.stack(outs[k])
    return (hp.reshape(bp, tp, d), hs.reshape(bs, ts, d),
            st("lat_p"), st("rope_p"), st("lat_s"), st("rope_s"),
            st("ret_p"), st("ret_s"), st("conv_p"), st("conv_s"), st("hg_p"), st("hg_s"))
```

```python
import functools
import math

import numpy as np
import jax
import jax.numpy as jnp
from jax import lax
from jax.experimental import pallas as pl
from jax.experimental.pallas import tpu as pltpu

F32 = jnp.float32
BF16 = jnp.bfloat16

D_MODEL = 1024
PAGE_SIZE = 128
RET_HEADS, RET_DK, RET_DV, RET_CHUNK = 4, 64, 128, 128
CONV_WIDTH, CONV_K = 512, 3
MLA_HEADS, MLA_Q_LORA, MLA_KV_LORA, MLA_NOPE, MLA_ROPE, MLA_V = 8, 384, 256, 64, 32, 64
MLA_SCALE = (MLA_NOPE + MLA_ROPE) ** -0.5
HGRN_HEADS, HGRN_DK, HGRN_DV = 4, 128, 128
D_FF = 4 * D_MODEL
ROPE_THETA = 10000.0
NORM_EPS = 1e-6
N_BRANCH = 4
NEG_BIG = -1e30
LOG2E = math.log2(math.e)

RET_W = RET_HEADS * RET_DV
MLA_W = MLA_HEADS * MLA_V
HGRN_W = HGRN_HEADS * HGRN_DV
IN_SIZES = (
    RET_HEADS * RET_DK, RET_HEADS * RET_DK, RET_W, RET_W,
    CONV_WIDTH, CONV_WIDTH, CONV_WIDTH,
    MLA_Q_LORA, MLA_KV_LORA, MLA_ROPE,
    HGRN_HEADS * HGRN_DK, HGRN_HEADS * HGRN_DK, HGRN_W, HGRN_W,
    N_BRANCH * D_MODEL,
)

LANES = 128
SUBLANES = 8
VMEM_LIMIT_BYTES = 56 * 1024 * 1024

GATE_W = N_BRANCH * D_MODEL
HG_BLOCK_W = 4 * HGRN_W
RET_BLOCK_W = 1536
CONV_BLOCK_W = 3 * CONV_WIDTH
MLA_BLOCK_W = 768
PACKED_W = GATE_W + HG_BLOCK_W + RET_BLOCK_W + CONV_BLOCK_W + MLA_BLOCK_W
MLA_HEAD_PAD = 128
HGRN_SUB = 16
HGRN_CHUNK = 64
HGRN_SAFE_LOG_DECAY = 160.0
KEY_PAD = 128


def _cparams(sem):
    return pltpu.CompilerParams(dimension_semantics=sem, vmem_limit_bytes=VMEM_LIMIT_BYTES)


def _rms(x, g):
    return x * lax.rsqrt(jnp.mean(x * x, axis=-1, keepdims=True) + NORM_EPS) * g


def _silu(x):
    return x * jax.nn.sigmoid(x)


def _dot(a, b):
    return jnp.dot(a, b, preferred_element_type=F32)


def _dot_nt(a, b):
    return lax.dot_general(a, b, (((1,), (1,)), ((), ())), preferred_element_type=F32)


def _dot_tn(a, b):
    return lax.dot_general(a, b, (((0,), (0,)), ((), ())), preferred_element_type=F32)


def _inproj_kernel(x_ref, g_ref, w_ref, o_ref, xn_ref):
    @pl.when(pl.program_id(1) == 0)
    def _():
        xn_ref[...] = _rms(x_ref[...], g_ref[...]).astype(BF16)

    o_ref[...] = _dot(xn_ref[...], w_ref[...])


def _inproj(x, g, w, *, tm, tn):
    n, d = x.shape
    c = w.shape[1]
    return pl.pallas_call(
        _inproj_kernel,
        out_shape=jax.ShapeDtypeStruct((n, c), F32),
        grid=(n // tm, c // tn),
        in_specs=[
            pl.BlockSpec((tm, d), lambda i, j: (i, 0)),
            pl.BlockSpec((1, d), lambda i, j: (0, 0)),
            pl.BlockSpec((d, tn), lambda i, j: (0, j)),
        ],
        out_specs=pl.BlockSpec((tm, tn), lambda i, j: (i, j)),
        scratch_shapes=[pltpu.VMEM((tm, d), BF16)],
        compiler_params=_cparams(("parallel", "arbitrary")),
        name="in_proj",
    )(x, g, w)


def _ret_kernel(x_ref, cos_ref, sin_ref, decay_ref, qdec_ref, kdec_ref, s0_ref,
                o_ref, sout_ref, s_ref, *, nt, cdec):
    t = pl.program_id(1)
    hw = RET_HEADS * RET_DK // 2
    per = RET_DK // 2

    @pl.when(t == 0)
    def _():
        s_ref[...] = jnp.zeros(s_ref.shape, F32)
        for h in range(RET_HEADS):
            s_ref[h, h * per:(h + 1) * per, :] = s0_ref[0, h, 0:per, :]
            s_ref[h, hw + h * per:hw + (h + 1) * per, :] = s0_ref[0, h, per:2 * per, :]

    x = x_ref[...]
    c = x.shape[0]
    cos = cos_ref[...]
    sin = sin_ref[...]
    q1, q2 = x[:, 0:hw], x[:, hw:2 * hw]
    k1, k2 = x[:, 2 * hw:3 * hw], x[:, 3 * hw:4 * hw]
    qr = jnp.concatenate([q1 * cos - q2 * sin, q1 * sin + q2 * cos], axis=1)
    kr = jnp.concatenate([k1 * cos - k2 * sin, k1 * sin + k2 * cos], axis=1) * (RET_DK ** -0.5)
    v = x[:, 4 * hw:4 * hw + RET_W]
    g = x[:, 4 * hw + RET_W:4 * hw + 2 * RET_W]
    if c < KEY_PAD:
        kr = jnp.concatenate([kr, jnp.zeros((KEY_PAD - c, kr.shape[1]), F32)], axis=0)
        v = jnp.concatenate([v, jnp.zeros((KEY_PAD - c, v.shape[1]), F32)], axis=0)
    kt = kr.T
    ktb = kt.astype(BF16)
    lane = lax.broadcasted_iota(jnp.int32, (1, 2 * hw), 1)
    head_of_lane = (lane % hw) // per
    for h in range(RET_HEADS):
        qm = jnp.where(head_of_lane == h, qr, 0.0).astype(BF16)
        vh = v[:, h * RET_DV:(h + 1) * RET_DV].astype(BF16)
        sh = s_ref[h]
        att = _dot(qm, ktb) * decay_ref[h]
        o = _dot(att.astype(BF16), vh) + _dot(qm, sh.astype(BF16)) * qdec_ref[h]
        s_ref[h] = sh * cdec[h] + _dot((kt * kdec_ref[h]).astype(BF16), vh)
        o = o * lax.rsqrt(jnp.mean(o * o, axis=-1, keepdims=True) + NORM_EPS)
        gh = g[:, h * RET_DV:(h + 1) * RET_DV]
        o_ref[:, h * RET_DV:(h + 1) * RET_DV] = (o * _silu(gh)).astype(o_ref.dtype)

    @pl.when(t == nt - 1)
    def _():
        for h in range(RET_HEADS):
            sout_ref[0, h, 0:per, :] = s_ref[h, h * per:(h + 1) * per, :]
            sout_ref[0, h, per:2 * per, :] = s_ref[h, hw + h * per:hw + (h + 1) * per, :]


def _retention(proj, cos, sin, state, *, b, t, odt):
    c = math.gcd(t, RET_CHUNK)
    nt = t // c
    ck = max(c, KEY_PAD)
    hw = RET_HEADS * RET_DK // 2
    hs = np.arange(RET_HEADS, dtype=np.float64)
    log_gamma = np.log1p(-np.exp2(-5.0 - hs))
    j = np.arange(c, dtype=np.float64)
    rel = j[:, None] - j[None, :]
    decay = np.where(rel >= 0, np.exp(log_gamma[:, None, None] * np.where(rel >= 0, rel, 0.0)), 0.0)
    decay_p = np.zeros((RET_HEADS, c, ck))
    decay_p[:, :, :c] = decay
    qdec = np.repeat(np.exp(log_gamma[:, None] * (j + 1.0))[:, :, None], RET_DV, axis=2)
    kdec = np.zeros((RET_HEADS, 1, ck))
    kdec[:, 0, :c] = np.exp(log_gamma[:, None] * (c - 1.0 - j))
    cdec = tuple(float(np.float32(np.exp(lg * c))) for lg in log_gamma)
    kern = functools.partial(_ret_kernel, nt=nt, cdec=cdec)
    return pl.pallas_call(
        kern,
        out_shape=(jax.ShapeDtypeStruct((b * t, RET_W), odt),
                   jax.ShapeDtypeStruct((b, RET_HEADS, RET_DK, RET_DV), F32)),
        grid=(b, nt),
        in_specs=[
            pl.BlockSpec((c, RET_BLOCK_W), lambda i, k: (i * nt + k, 4)),
            pl.BlockSpec((c, hw), lambda i, k: (k, 0)),
            pl.BlockSpec((c, hw), lambda i, k: (k, 0)),
            pl.BlockSpec((RET_HEADS, c, ck), lambda i, k: (0, 0, 0)),
            pl.BlockSpec((RET_HEADS, c, RET_DV), lambda i, k: (0, 0, 0)),
            pl.BlockSpec((RET_HEADS, 1, ck), lambda i, k: (0, 0, 0)),
            pl.BlockSpec((1, RET_HEADS, RET_DK, RET_DV), lambda i, k: (i, 0, 0, 0)),
        ],
        out_specs=(pl.BlockSpec((c, RET_W), lambda i, k: (i * nt + k, 0)),
                   pl.BlockSpec((1, RET_HEADS, RET_DK, RET_DV), lambda i, k: (i, 0, 0, 0))),
        scratch_shapes=[pltpu.VMEM((RET_HEADS, RET_HEADS * RET_DK, RET_DV), F32)],
        compiler_params=_cparams(("parallel", "arbitrary")),
        name="retention",
    )(proj, cos, sin, jnp.asarray(decay_p, F32), jnp.asarray(qdec, F32), jnp.asarray(kdec, F32), state)


def _conv_kernel(x_ref, w_ref, buf_ref, o_ref, bout_ref, carry_ref, *, nt):
    t = pl.program_id(1)
    w = CONV_WIDTH

    @pl.when(t == 0)
    def _():
        carry_ref[...] = jnp.zeros(carry_ref.shape, F32)
        carry_ref[SUBLANES - 2:SUBLANES, :] = buf_ref[0]

    x = x_ref[...]
    rows = x.shape[0]
    gate_b, gate_c, xin = x[:, 0:w], x[:, w:2 * w], x[:, 2 * w:3 * w]
    u = gate_c * xin
    row = lax.broadcasted_iota(jnp.int32, (rows, 1), 0)
    p1 = carry_ref[SUBLANES - 1:SUBLANES, :]
    p2 = carry_ref[SUBLANES - 2:SUBLANES - 1, :]
    u1 = jnp.where(row == 0, p1, pltpu.roll(u, 1, 0))
    u2 = jnp.where(row == 0, p2, jnp.where(row == 1, p1, pltpu.roll(u, 2, 0)))
    y = u2 * w_ref[0:1, :] + u1 * w_ref[1:2, :] + u * w_ref[2:3, :]
    o_ref[...] = (gate_b * y).astype(o_ref.dtype)
    carry_ref[...] = u[rows - SUBLANES:rows, :]

    @pl.when(t == nt - 1)
    def _():
        bout_ref[0] = u[rows - 2:rows, :]


def _short_conv(proj, w, buf, *, b, t, rows, odt):
    nt = t // rows
    kern = functools.partial(_conv_kernel, nt=nt)
    return pl.pallas_call(
        kern,
        out_shape=(jax.ShapeDtypeStruct((b * t, CONV_WIDTH), odt),
                   jax.ShapeDtypeStruct((b, CONV_K - 1, CONV_WIDTH), F32)),
        grid=(b, nt),
        in_specs=[
            pl.BlockSpec((rows, CONV_BLOCK_W), lambda i, k: (i * nt + k, 5)),
            pl.BlockSpec((CONV_K, CONV_WIDTH), lambda i, k: (0, 0)),
            pl.BlockSpec((1, CONV_K - 1, CONV_WIDTH), lambda i, k: (i, 0, 0)),
        ],
        out_specs=(pl.BlockSpec((rows, CONV_WIDTH), lambda i, k: (i * nt + k, 0)),
                   pl.BlockSpec((1, CONV_K - 1, CONV_WIDTH), lambda i, k: (i, 0, 0))),
        scratch_shapes=[pltpu.VMEM((SUBLANES, CONV_WIDTH), F32)],
        compiler_params=_cparams(("parallel", "arbitrary")),
        name="short_conv",
    )(proj, w, buf)


def _split3(x):
    hi = x.astype(BF16)
    r = x - hi.astype(F32)
    mid = r.astype(BF16)
    lo = (r - mid.astype(F32)).astype(BF16)
    return hi, mid, lo


def _hgrn_intra_exact(gcum, qh, kk, hv, g_ref, k_ref, oi_ref):
    ell = gcum.shape[0]
    dk = HGRN_DK
    g_ref[...] = gcum
    k_ref[...] = kk
    nsub = ell // HGRN_SUB
    lane = lax.broadcasted_iota(jnp.int32, (1, KEY_PAD), 1)
    key_row = lax.broadcasted_iota(jnp.int32, (ell, 1), 0)
    sub_row = lax.broadcasted_iota(jnp.int32, (HGRN_SUB, 1), 0)
    a_blocks = [[None] * nsub for _ in range(HGRN_HEADS)]
    for j in range(nsub):
        r0 = j * HGRN_SUB
        gq = gcum[r0:r0 + HGRN_SUB, :]
        qq = qh[r0:r0 + HGRN_SUB, :]

        def diag_body(s, carry, r0=r0, gq=gq, qq=qq):
            gs = g_ref[pl.ds(r0 + s, 1), :]
            ks = k_ref[pl.ds(r0 + s, 1), :]
            valid = sub_row >= s
            p = jnp.where(valid, jnp.exp(jnp.minimum(gq - gs, 0.0)) * qq * ks, 0.0)
            out = []
            for h in range(HGRN_HEADS):
                a = jnp.sum(p[:, h * dk:(h + 1) * dk], axis=-1, keepdims=True)
                out.append(carry[h] + jnp.where(lane == r0 + s, a, 0.0))
            return tuple(out)

        init = tuple(jnp.zeros((HGRN_SUB, KEY_PAD), F32) for _ in range(HGRN_HEADS))
        diag = lax.fori_loop(0, HGRN_SUB, diag_body, init)
        if j > 0:
            gb = gcum[r0 - 1:r0, :]
            q_t = (qq * jnp.exp(gq - gb)).astype(BF16)
            k_t = jnp.where(key_row < r0, kk * jnp.exp(jnp.minimum(gb - gcum, 0.0)), 0.0)
            if ell < KEY_PAD:
                k_t = jnp.concatenate([k_t, jnp.zeros((KEY_PAD - ell, k_t.shape[1]), F32)], axis=0)
            k_t = k_t.astype(BF16)
        for h in range(HGRN_HEADS):
            blk = diag[h]
            if j > 0:
                blk = blk + _dot_nt(q_t[:, h * dk:(h + 1) * dk], k_t[:, h * dk:(h + 1) * dk])
            a_blocks[h][j] = blk
    for h in range(HGRN_HEADS):
        a_h = a_blocks[h][0] if nsub == 1 else jnp.concatenate(a_blocks[h], axis=0)
        vh = hv[:, h * HGRN_DV:(h + 1) * HGRN_DV]
        if ell < KEY_PAD:
            vh = jnp.concatenate([vh, jnp.zeros((KEY_PAD - ell, HGRN_DV), F32)], axis=0)
        oi_ref[h] = _dot(a_h.astype(BF16), vh.astype(BF16))


def _hgrn_intra_fast(gcum, g_end, qh, kk, hv, oi_ref):
    ell = gcum.shape[0]
    dk = HGRN_DK
    gm = 0.5 * g_end
    q_t = (qh * jnp.exp(gcum - gm)).astype(BF16)
    k_t = (kk * jnp.exp(gm - gcum)).astype(BF16)
    causal = lax.broadcasted_iota(jnp.int32, (ell, ell), 0) >= lax.broadcasted_iota(jnp.int32, (ell, ell), 1)
    for h in range(HGRN_HEADS):
        a = jnp.where(causal, _dot_nt(q_t[:, h * dk:(h + 1) * dk], k_t[:, h * dk:(h + 1) * dk]), 0.0)
        oi_ref[h] = _dot(a.astype(BF16), hv[:, h * HGRN_DV:(h + 1) * HGRN_DV].astype(BF16))


def _hgrn_kernel(x_ref, lb_ref, gn_ref, s0_ref, o_ref, sout_ref,
                 st_ref, g_ref, k_ref, oi_ref, *, nt, chunk):
    t = pl.program_id(1)
    w = HGRN_W
    dk = HGRN_DK

    @pl.when(t == 0)
    def _():
        for h in range(HGRN_HEADS):
            st_ref[h] = s0_ref[0, h].T

    rows = x_ref.shape[0]
    lb = lb_ref[...]
    gn = gn_ref[...]
    ri = lax.broadcasted_iota(jnp.int32, (chunk, chunk), 0)
    ci = lax.broadcasted_iota(jnp.int32, (chunk, chunk), 1)
    tri = jnp.where(ri >= ci, 1.0, 0.0).astype(BF16)
    parts = []
    for c in range(max(rows // chunk, 1)):
        if rows >= chunk:
            x = x_ref[c * chunk:(c + 1) * chunk, :]
            live = chunk
        else:
            x = jnp.concatenate([x_ref[...], jnp.zeros((chunk - rows, x_ref.shape[1]), F32)], axis=0)
            live = rows
        hq, hf, hv, hg = x[:, 0:w], x[:, w:2 * w], x[:, 2 * w:3 * w], x[:, 3 * w:4 * w]
        f_gate = lb + (1.0 - lb) * jax.nn.sigmoid(hf)
        log_f = jnp.log(f_gate)
        kk = 1.0 - f_gate
        if live < chunk:
            ok = lax.broadcasted_iota(jnp.int32, (chunk, 1), 0) < live
            log_f = jnp.where(ok, log_f, 0.0)
            kk = jnp.where(ok, kk, 0.0)
        qh = _silu(hq)
        p_hi, p_mid, p_lo = _split3(log_f)
        gcum = _dot(tri, p_hi) + _dot(tri, p_mid) + _dot(tri, p_lo)
        parts.append((gcum, gcum[chunk - 1:chunk, :], qh, kk, hv, hg, live))

    g_min = parts[0][1]
    for prt in parts[1:]:
        g_min = jnp.minimum(g_min, prt[1])
    safe = jnp.min(g_min) > -HGRN_SAFE_LOG_DECAY

    @pl.when(safe)
    def _():
        for c, (gcum, g_end, qh, kk, hv, _, _) in enumerate(parts):
            _hgrn_intra_fast(gcum, g_end, qh, kk, hv, oi_ref.at[c])

    @pl.when(jnp.logical_not(safe))
    def _():
        for c, (gcum, g_end, qh, kk, hv, _, _) in enumerate(parts):
            _hgrn_intra_exact(gcum, qh, kk, hv, g_ref, k_ref, oi_ref.at[c])

    for c, (gcum, g_end, qh, kk, hv, hg, live) in enumerate(parts):
        q_in = (qh * jnp.exp(gcum)).astype(BF16)
        k_out = (kk * jnp.exp(g_end - gcum)).astype(BF16)
        e_end = jnp.exp(g_end)
        for h in range(HGRN_HEADS):
            sl = slice(h * dk, (h + 1) * dk)
            st = st_ref[h]
            o = oi_ref[c, h] + _dot_nt(q_in[:, sl], st.astype(BF16))
            st_ref[h] = st * e_end[:, sl] + _dot_tn(hv[:, h * HGRN_DV:(h + 1) * HGRN_DV].astype(BF16), k_out[:, sl])
            o = _rms(o, gn) * _silu(hg[:, h * HGRN_DV:(h + 1) * HGRN_DV])
            o_ref[c * chunk:c * chunk + live, h * HGRN_DV:(h + 1) * HGRN_DV] = o[0:live, :].astype(o_ref.dtype)

    @pl.when(t == nt - 1)
    def _():
        for h in range(HGRN_HEADS):
            sout_ref[0, h] = st_ref[h].T


def _hgrn(proj, lb, gnorm, state, *, b, t, rows, odt):
    nt = t // rows
    chunk = max(min(rows, HGRN_CHUNK), HGRN_SUB)
    nchunk = max(rows // chunk, 1)
    kern = functools.partial(_hgrn_kernel, nt=nt, chunk=chunk)
    return pl.pallas_call(
        kern,
        out_shape=(jax.ShapeDtypeStruct((b * t, HGRN_W), odt),
                   jax.ShapeDtypeStruct((b, HGRN_HEADS, HGRN_DK, HGRN_DV), F32)),
        grid=(b, nt),
        in_specs=[
            pl.BlockSpec((rows, HG_BLOCK_W), lambda i, k: (i * nt + k, 2)),
            pl.BlockSpec((1, HGRN_W), lambda i, k: (0, 0)),
            pl.BlockSpec((1, HGRN_DV), lambda i, k: (0, 0)),
            pl.BlockSpec((1, HGRN_HEADS, HGRN_DK, HGRN_DV), lambda i, k: (i, 0, 0, 0)),
        ],
        out_specs=(pl.BlockSpec((rows, HGRN_W), lambda i, k: (i * nt + k, 0)),
                   pl.BlockSpec((1, HGRN_HEADS, HGRN_DK, HGRN_DV), lambda i, k: (i, 0, 0, 0))),
        scratch_shapes=[pltpu.VMEM((HGRN_HEADS, HGRN_DV, HGRN_DK), F32),
                        pltpu.VMEM((chunk, HGRN_W), F32),
                        pltpu.VMEM((chunk, HGRN_W), F32),
                        pltpu.VMEM((nchunk, HGRN_HEADS, chunk, HGRN_DV), F32)],
        compiler_params=_cparams(("parallel", "arbitrary")),
        name="hgrn2",
    )(proj, lb, gnorm, state)


def _mla_prep_kernel(x_ref, qn_ref, kvn_ref, wqn_ref, wqp_ref, wqr_ref, eq_ref,
                     cq_ref, sq_ref, ck_ref, sk_ref, *rest, with_kv):
    if with_kv:
        wuk_ref, ek_ref, wuv_ref, q_ref, lat_ref, kpe_ref, kc_ref, v_ref = rest
    else:
        q_ref, lat_ref, kpe_ref = rest
    x = x_ref[...]
    mcq = x[:, 0:MLA_Q_LORA]
    mckv = x[:, MLA_Q_LORA:MLA_Q_LORA + MLA_KV_LORA]
    o = MLA_Q_LORA + MLA_KV_LORA
    mkr = x[:, o:o + MLA_ROPE]
    mkr_rot = x[:, o + MLA_ROPE:o + 2 * MLA_ROPE]
    xq = _rms(mcq, qn_ref[...]).astype(BF16)
    q_pe = _dot(xq, wqp_ref[...]) * cq_ref[...] + _dot(xq, wqr_ref[...]) * sq_ref[...]
    q_cat = _dot(xq, wqn_ref[...]) + _dot(q_pe.astype(BF16), eq_ref[...])
    q_ref[...] = (q_cat * (MLA_SCALE * LOG2E)).astype(q_ref.dtype)
    ckv = _rms(mckv, kvn_ref[...])
    lat_ref[...] = ckv
    kpe = mkr * ck_ref[...] + mkr_rot * sk_ref[...]
    kpe_ref[...] = kpe
    if with_kv:
        cb = ckv.astype(BF16)
        kc_ref[...] = (_dot(cb, wuk_ref[...]) + _dot(kpe.astype(BF16), ek_ref[...])).astype(BF16)
        v_ref[0] = _dot_nt(wuv_ref[...], cb).astype(BF16)


def _mla_prep(proj, lw, tabs, *, n, t, tm, with_kv):
    rows = min(tm, t)
    nt = t // rows
    hp = MLA_HEADS * MLA_HEAD_PAD
    qp_w = MLA_HEADS * MLA_ROPE
    full = lambda r, c: pl.BlockSpec((r, c), lambda i: (0, 0))
    tab = lambda c: pl.BlockSpec((rows, c), lambda i: (i % nt, 0))
    in_specs = [
        pl.BlockSpec((rows, MLA_BLOCK_W), lambda i: (i, 12)),
        full(1, MLA_Q_LORA), full(1, MLA_KV_LORA),
        full(MLA_Q_LORA, hp), full(MLA_Q_LORA, qp_w), full(MLA_Q_LORA, qp_w), full(qp_w, hp),
        tab(qp_w), tab(qp_w), tab(MLA_ROPE), tab(MLA_ROPE),
    ]
    args = [proj, lw["q_norm"], lw["kv_norm"], lw["wq_nope"], lw["wq_pe"], lw["wq_rot"], lw["e_q"],
            tabs["cos_q"], tabs["sin_q"], tabs["cos_k"], tabs["sin_k"]]
    q_dtype = BF16 if with_kv else F32
    out_shape = [jax.ShapeDtypeStruct((n, hp), q_dtype),
                 jax.ShapeDtypeStruct((n, MLA_KV_LORA), F32),
                 jax.ShapeDtypeStruct((n, MLA_ROPE), F32)]
    out_specs = [pl.BlockSpec((rows, hp), lambda i: (i, 0)),
                 pl.BlockSpec((rows, MLA_KV_LORA), lambda i: (i, 0)),
                 pl.BlockSpec((rows, MLA_ROPE), lambda i: (i, 0))]
    if with_kv:
        in_specs += [full(MLA_KV_LORA, hp), full(MLA_ROPE, hp), full(MLA_W, MLA_KV_LORA)]
        args += [lw["wuk_pad"], lw["e_k"], lw["wuv_t"]]
        out_shape += [jax.ShapeDtypeStruct((n, hp), BF16), jax.ShapeDtypeStruct((n // t, MLA_W, t), BF16)]
        out_specs += [pl.BlockSpec((rows, hp), lambda i: (i, 0)),
                      pl.BlockSpec((1, MLA_W, rows), lambda i: (i // nt, 0, i % nt))]
    return pl.pallas_call(
        functools.partial(_mla_prep_kernel, with_kv=with_kv),
        out_shape=tuple(out_shape),
        grid=(n // rows,),
        in_specs=in_specs,
        out_specs=tuple(out_specs),
        compiler_params=_cparams(("parallel",)),
        name="mla_prep",
    )(*args)


def _flash_kernel(q_ref, k_ref, vt_ref, o_ref, m_ref, l_ref, acc_ref, *, tq, tk):
    qi = pl.program_id(1)
    kj = pl.program_id(2)
    hp = MLA_HEAD_PAD

    @pl.when(kj == 0)
    def _():
        m_ref[...] = jnp.full(m_ref.shape, NEG_BIG, F32)
        l_ref[...] = jnp.zeros(l_ref.shape, F32)
        acc_ref[...] = jnp.zeros(acc_ref.shape, F32)

    def step(mask):
        for h in range(MLA_HEADS):
            q = q_ref[:, h * hp:(h + 1) * hp]
            k = k_ref[:, h * hp:(h + 1) * hp]
            s = _dot_nt(k, q)
            if mask is not None:
                s = jnp.where(mask, s, NEG_BIG)
            m_old = m_ref[h]
            m_new = jnp.maximum(m_old, jnp.max(s, axis=0, keepdims=True))
            alpha = jnp.exp2(m_old - m_new)
            p = jnp.exp2(s - m_new)
            l_ref[h] = l_ref[h] * alpha + jnp.sum(p, axis=0, keepdims=True)
            vt = vt_ref[0, h * MLA_V:(h + 1) * MLA_V, :]
            acc_ref[h] = acc_ref[h] * alpha + _dot(vt, p.astype(BF16))
            m_ref[h] = m_new

    @pl.when(kj < qi)
    def _():
        step(None)

    @pl.when(kj == qi)
    def _():
        step(lax.broadcasted_iota(jnp.int32, (tk, tq), 0) <= lax.broadcasted_iota(jnp.int32, (tk, tq), 1))
        for h in range(MLA_HEADS):
            acc_ref[h] = acc_ref[h] / l_ref[h]
        o_ref[...] = acc_ref[...].reshape(MLA_W, tq).T.astype(o_ref.dtype)


def _flash(q, k, vt, *, b, t, tq, tk, odt):
    nq, nk = t // tq, t // tk
    hp = MLA_HEADS * MLA_HEAD_PAD
    kern = functools.partial(_flash_kernel, tq=tq, tk=tk)
    return pl.pallas_call(
        kern,
        out_shape=jax.ShapeDtypeStruct((b * t, MLA_W), odt),
        grid=(b, nq, nk),
        in_specs=[
            pl.BlockSpec((tq, hp), lambda i, a, c: (i * nq + a, 0)),
            pl.BlockSpec((tk, hp), lambda i, a, c: (i * nk + jnp.minimum(c, a), 0)),
            pl.BlockSpec((1, MLA_W, tk), lambda i, a, c: (i, 0, jnp.minimum(c, a))),
        ],
        out_specs=pl.BlockSpec((tq, MLA_W), lambda i, a, c: (i * nq + a, 0)),
        scratch_shapes=[pltpu.VMEM((MLA_HEADS, 1, tq), F32),
                        pltpu.VMEM((MLA_HEADS, 1, tq), F32),
                        pltpu.VMEM((MLA_HEADS, MLA_V, tq), F32)],
        compiler_params=_cparams(("parallel", "parallel", "arbitrary")),
        name="mla_flash",
    )(q, k, vt)


def _paged_kernel(pt_ref, q_ref, lat_new_ref, kpe_new_ref, wabs_ref, wuv_ref, *rest,
                  ng, pg, t, nb):
    npg = nb * pg
    lat_refs = rest[0:npg]
    pe_refs = rest[npg:2 * npg]
    o_ref = rest[2 * npg]
    ql_ref, qp_ref, m_ref, l_ref, acc_ref, c_ref, rt_ref = rest[2 * npg + 1:]
    g = pl.program_id(1)
    hp = MLA_HEAD_PAD
    rows = MLA_HEADS * t

    @pl.when(g == 0)
    def _():
        lane_head = lax.broadcasted_iota(jnp.int32, (1, MLA_HEADS * hp), 1) // hp
        for e in range(nb):
            q = q_ref[e]
            q_exp = jnp.concatenate(
                [jnp.where(lane_head == h, q, 0.0) for h in range(MLA_HEADS)], axis=0).astype(BF16)
            q_abs = _dot(q_exp, wabs_ref[...])
            ql_ref[e] = q_abs[:, 0:MLA_KV_LORA].astype(BF16)
            qp_ref[e] = q_abs[:, MLA_KV_LORA:MLA_KV_LORA + LANES].astype(BF16)
        m_ref[...] = jnp.full(m_ref.shape, NEG_BIG, F32)
        l_ref[...] = jnp.zeros(l_ref.shape, F32)
        acc_ref[...] = jnp.zeros(acc_ref.shape, F32)

    def update(e, s, c):
        m_old = m_ref[e]
        m_new = jnp.maximum(m_old, jnp.max(s, axis=-1, keepdims=True))
        alpha = jnp.exp2(m_old - m_new)
        p = jnp.exp2(s - m_new)
        l_ref[e] = l_ref[e] * alpha + jnp.sum(p, axis=-1, keepdims=True)
        acc_ref[e] = acc_ref[e] * alpha + _dot(p.astype(BF16), c)
        m_ref[e] = m_new

    for e in range(nb):
        for i in range(pg):
            c_ref[e, i * PAGE_SIZE:(i + 1) * PAGE_SIZE, :] = lat_refs[e * pg + i][0, 0].astype(BF16)
            rt_ref[e, :, i * PAGE_SIZE:(i + 1) * PAGE_SIZE] = pe_refs[e * pg + i][0, 0].astype(BF16)
    for e in range(nb):
        c_all = c_ref[e]
        update(e, _dot_nt(ql_ref[e], c_all) + _dot(qp_ref[e, :, 0:MLA_ROPE], rt_ref[e]), c_all)

    @pl.when(g == ng - 1)
    def _():
        qtok = lax.broadcasted_iota(jnp.int32, (rows, KEY_PAD), 0) % t
        kidx = lax.broadcasted_iota(jnp.int32, (rows, KEY_PAD), 1)
        lane_head = lax.broadcasted_iota(jnp.int32, (1, MLA_W), 1) // MLA_V
        for e in range(nb):
            cn = jnp.concatenate([lat_new_ref[e], jnp.zeros((KEY_PAD - t, MLA_KV_LORA), F32)], axis=0).astype(BF16)
            rn = jnp.concatenate([kpe_new_ref[e], jnp.zeros((KEY_PAD - t, MLA_ROPE), F32)], axis=0).astype(BF16)
            s = _dot_nt(ql_ref[e], cn) + _dot_nt(qp_ref[e, :, 0:MLA_ROPE], rn)
            update(e, jnp.where(kidx <= qtok, s, NEG_BIG), cn)
            o_lat = acc_ref[e] / l_ref[e]
            res = _dot(o_lat.astype(BF16), wuv_ref[...])
            out = jnp.zeros((t, MLA_W), F32)
            for h in range(MLA_HEADS):
                out = out + jnp.where(lane_head == h, res[h * t:(h + 1) * t, :], 0.0)
            o_ref[e] = out


def _paged_attention(page_table, q, lat_new, kpe_new, wabs, wuv, lat_pool, pe_pool, *, layer, pg, nb):
    b, t, hp = q.shape
    n_pages = page_table.shape[1]
    ng = n_pages // pg
    rows = MLA_HEADS * t

    def page_spec(r, c, e, i):
        return pl.BlockSpec((1, 1, r, c), lambda bi, g, pt: (layer, pt[bi * nb + e, g * pg + i], 0, 0))

    in_specs = [
        pl.BlockSpec((nb, t, hp), lambda bi, g, pt: (bi, 0, 0)),
        pl.BlockSpec((nb, t, MLA_KV_LORA), lambda bi, g, pt: (bi, 0, 0)),
        pl.BlockSpec((nb, t, MLA_ROPE), lambda bi, g, pt: (bi, 0, 0)),
        pl.BlockSpec(wabs.shape, lambda bi, g, pt: (0, 0)),
        pl.BlockSpec(wuv.shape, lambda bi, g, pt: (0, 0)),
    ]
    in_specs += [page_spec(PAGE_SIZE, MLA_KV_LORA, e, i) for e in range(nb) for i in range(pg)]
    in_specs += [page_spec(MLA_ROPE, PAGE_SIZE, e, i) for e in range(nb) for i in range(pg)]
    kern = functools.partial(_paged_kernel, ng=ng, pg=pg, t=t, nb=nb)
    return pl.pallas_call(
        kern,
        out_shape=jax.ShapeDtypeStruct((b, t, MLA_W), F32),
        grid_spec=pltpu.PrefetchScalarGridSpec(
            num_scalar_prefetch=1,
            grid=(b // nb, ng),
            in_specs=in_specs,
            out_specs=pl.BlockSpec((nb, t, MLA_W), lambda bi, g, pt: (bi, 0, 0)),
            scratch_shapes=[pltpu.VMEM((nb, rows, MLA_KV_LORA), BF16),
                            pltpu.VMEM((nb, rows, LANES), BF16),
                            pltpu.VMEM((nb, rows, 1), F32),
                            pltpu.VMEM((nb, rows, 1), F32),
                            pltpu.VMEM((nb, rows, MLA_KV_LORA), F32),
                            pltpu.VMEM((nb, pg * PAGE_SIZE, MLA_KV_LORA), BF16),
                            pltpu.VMEM((nb, MLA_ROPE, pg * PAGE_SIZE), BF16)],
        ),
        compiler_params=_cparams(("parallel", "arbitrary")),
        name="mla_paged",
    )(page_table, q, lat_new, kpe_new, wabs, wuv, *([lat_pool] * (nb * pg)), *([pe_pool] * (nb * pg)))


def _merge_kernel(h_ref, gate_ref, b0_ref, b1_ref, b2_ref, b3_ref, wbo_ref, wout_ref, o_ref):
    branches = (b0_ref, b1_ref, b2_ref, b3_ref)
    merged = None
    off = 0
    for i, br in enumerate(branches):
        wd = br.shape[1]
        proj = _dot(br[...].astype(BF16), wbo_ref[off:off + wd, :])
        term = jax.nn.sigmoid(gate_ref[:, i * D_MODEL:(i + 1) * D_MODEL]) * proj
        merged = term if merged is None else merged + term
        off += wd
    o_ref[...] = h_ref[...] + _dot(merged.astype(BF16), wout_ref[...])


def _merge(h, proj, branches, wbo, wout, *, tm):
    n = h.shape[0]
    row = lambda c: pl.BlockSpec((tm, c), lambda i: (i, 0))
    return pl.pallas_call(
        _merge_kernel,
        out_shape=jax.ShapeDtypeStruct((n, D_MODEL), F32),
        grid=(n // tm,),
        in_specs=[row(D_MODEL), row(GATE_W)] + [row(br.shape[1]) for br in branches]
        + [pl.BlockSpec(wbo.shape, lambda i: (0, 0)), pl.BlockSpec(wout.shape, lambda i: (0, 0))],
        out_specs=row(D_MODEL),
        compiler_params=_cparams(("parallel",)),
        name="branch_merge",
    )(h, proj, *branches, wbo, wout)


def _ffn_kernel(h_ref, g_ref, w1_ref, w2_ref, gf_ref, o_ref, *, tf, final_norm):
    h = h_ref[...]
    xn = _rms(h, g_ref[...]).astype(BF16)
    acc = None
    for j in range(w1_ref.shape[1] // tf):
        u = _dot(xn, w1_ref[:, j * tf:(j + 1) * tf])
        u = jnp.square(jnp.maximum(u, 0.0))
        part = _dot(u.astype(BF16), w2_ref[j * tf:(j + 1) * tf, :])
        acc = part if acc is None else acc + part
    out = h + acc
    if final_norm:
        out = _rms(out, gf_ref[...])
    o_ref[...] = out


def _ffn(h, g, w1, w2, gf, *, tm, tf, final_norm):
    n, d = h.shape
    f = w1.shape[1]
    kern = functools.partial(_ffn_kernel, tf=tf, final_norm=final_norm)
    return pl.pallas_call(
        kern,
        out_shape=jax.ShapeDtypeStruct((n, d), F32),
        grid=(n // tm,),
        in_specs=[
            pl.BlockSpec((tm, d), lambda i: (i, 0)),
            pl.BlockSpec((1, d), lambda i: (0, 0)),
            pl.BlockSpec((d, f), lambda i: (0, 0)),
            pl.BlockSpec((f, d), lambda i: (0, 0)),
            pl.BlockSpec((1, d), lambda i: (0, 0)),
        ],
        out_specs=pl.BlockSpec((tm, d), lambda i: (i, 0)),
        compiler_params=_cparams(("parallel",)),
        name="relu2_mlp",
    )(h, g, w1, w2, gf)


def _rot_partner(w, half):
    return jnp.concatenate([-w[..., half:], w[..., :half]], axis=-1)


def _pack_layer(w_in_l, conv_w_l, q_norm_l, w_uq_l, kv_norm_l, w_uk_l, w_uv_l, hgrn_norm_l,
                w_bo_l, w_out_l, g_mix_l, g_ffn_l, w_ff1_l, w_ff2_l):
    d = w_in_l.shape[0]
    offs = np.cumsum((0,) + IN_SIZES)
    col = lambda i: w_in_l[:, int(offs[i]):int(offs[i + 1])]
    rq, rk, rv, rg, cb, cc, cx, mcq, mckv, mkr, hq, hf, hi, hg, mg = [col(i) for i in range(15)]

    def halves(w):
        w4 = w.reshape(d, RET_HEADS, 2, RET_DK // 2)
        return w4[:, :, 0, :].reshape(d, -1), w4[:, :, 1, :].reshape(d, -1)

    rq1, rq2 = halves(rq)
    rk1, rk2 = halves(rk)
    mkr_rot = _rot_partner(mkr, MLA_ROPE // 2)
    pad = jnp.zeros((d, MLA_BLOCK_W - MLA_Q_LORA - MLA_KV_LORA - 2 * MLA_ROPE), w_in_l.dtype)
    packed = jnp.concatenate([mg, hq, hf, hi, hg, rq1, rq2, rk1, rk2, rv, rg, cb, cc, cx,
                              mcq, mckv, mkr, mkr_rot, pad], axis=1).astype(BF16)

    qd = MLA_NOPE + MLA_ROPE
    wq = w_uq_l.reshape(MLA_Q_LORA, MLA_HEADS, qd)
    wq_nope = jnp.concatenate(
        [wq[:, :, :MLA_NOPE], jnp.zeros((MLA_Q_LORA, MLA_HEADS, MLA_HEAD_PAD - MLA_NOPE), wq.dtype)],
        axis=2).reshape(MLA_Q_LORA, MLA_HEADS * MLA_HEAD_PAD)
    wq_pe = wq[:, :, MLA_NOPE:]
    wq_rot = _rot_partner(wq_pe, MLA_ROPE // 2)
    e_q = np.zeros((MLA_HEADS * MLA_ROPE, MLA_HEADS * MLA_HEAD_PAD), np.float32)
    e_k = np.zeros((MLA_ROPE, MLA_HEADS * MLA_HEAD_PAD), np.float32)
    for h in range(MLA_HEADS):
        for i in range(MLA_ROPE):
            e_q[h * MLA_ROPE + i, h * MLA_HEAD_PAD + MLA_NOPE + i] = 1.0
            e_k[i, h * MLA_HEAD_PAD + MLA_NOPE + i] = 1.0
    wuk_pad = jnp.concatenate(
        [w_uk_l, jnp.zeros((MLA_KV_LORA, MLA_HEADS, MLA_HEAD_PAD - MLA_NOPE), w_uk_l.dtype)],
        axis=2).reshape(MLA_KV_LORA, MLA_HEADS * MLA_HEAD_PAD)
    wabs = jnp.zeros((MLA_HEADS, MLA_HEAD_PAD, MLA_KV_LORA + LANES), F32)
    wabs = wabs.at[:, :MLA_NOPE, :MLA_KV_LORA].set(jnp.transpose(w_uk_l, (1, 2, 0)))
    wabs = wabs.at[:, MLA_NOPE:MLA_NOPE + MLA_ROPE, MLA_KV_LORA:MLA_KV_LORA + MLA_ROPE].set(
        jnp.broadcast_to(jnp.eye(MLA_ROPE, dtype=F32), (MLA_HEADS, MLA_ROPE, MLA_ROPE)))
    wabs = wabs.reshape(MLA_HEADS * MLA_HEAD_PAD, MLA_KV_LORA + LANES)
    return {
        "w_in": packed,
        "g_mix": g_mix_l.reshape(1, d),
        "conv_w": conv_w_l,
        "q_norm": q_norm_l.reshape(1, -1),
        "kv_norm": kv_norm_l.reshape(1, -1),
        "wq_nope": wq_nope.astype(BF16),
        "wq_pe": wq_pe.reshape(MLA_Q_LORA, -1).astype(BF16),
        "wq_rot": wq_rot.reshape(MLA_Q_LORA, -1).astype(BF16),
        "e_q": jnp.asarray(e_q, BF16),
        "e_k": jnp.asarray(e_k, BF16),
        "wuk_pad": wuk_pad.astype(BF16),
        "wuv": w_uv_l.reshape(MLA_KV_LORA, MLA_W).astype(BF16),
        "wuv_t": w_uv_l.reshape(MLA_KV_LORA, MLA_W).T.astype(BF16),
        "wabs": wabs.astype(BF16),
        "hgrn_norm": hgrn_norm_l.reshape(1, -1),
        "w_bo": w_bo_l.astype(BF16),
        "w_out": w_out_l.astype(BF16),
        "g_ffn": g_ffn_l.reshape(1, d),
        "w_ff1": w_ff1_l.astype(BF16),
        "w_ff2": w_ff2_l.astype(BF16),
    }


def _rope_tables(pos):
    def cs(half):
        inv = ROPE_THETA ** (-jnp.arange(half, dtype=F32) / half)
        ang = pos.astype(F32)[:, None] * inv[None, :]
        return jnp.cos(ang), jnp.sin(ang)

    c_ret, s_ret = cs(RET_DK // 2)
    c_pe, s_pe = cs(MLA_ROPE // 2)
    two = lambda a: jnp.concatenate([a, a], axis=1)
    return {
        "cos_ret": jnp.tile(c_ret, (1, RET_HEADS)), "sin_ret": jnp.tile(s_ret, (1, RET_HEADS)),
        "cos_q": jnp.tile(two(c_pe), (1, MLA_HEADS)), "sin_q": jnp.tile(two(s_pe), (1, MLA_HEADS)),
        "cos_k": two(c_pe), "sin_k": two(s_pe),
    }


def _tiles(t):
    return {"conv": min(512, t), "hgrn": min(256, t), "prep": min(512, t), "tq": min(512, t), "tk": min(512, t)}


def _trunk_layer(h, tabs, ret_state, conv_buf, hgrn_state, lb, lw, *, b, t, attend, final_gain):
    n = b * t
    tl = _tiles(t)
    odt = BF16 if t % 16 == 0 else F32
    proj = _inproj(h, lw["g_mix"], lw["w_in"], tm=min(n, 2048), tn=768)
    o_ret, ret_new = _retention(proj, tabs["cos_ret"], tabs["sin_ret"], ret_state, b=b, t=t, odt=odt)
    o_conv, conv_new = _short_conv(proj, lw["conv_w"], conv_buf, b=b, t=t, rows=tl["conv"], odt=odt)
    o_h, hgrn_new = _hgrn(proj, lb, lw["hgrn_norm"], hgrn_state, b=b, t=t, rows=tl["hgrn"], odt=odt)
    o_mla, ckv, kpe = attend(proj, lw, tabs, tl, odt)
    h = _merge(h, proj, (o_ret, o_conv, o_mla, o_h), lw["w_bo"], lw["w_out"], tm=min(n, 256))
    gf = final_gain if final_gain is not None else lw["g_ffn"]
    h = _ffn(h, lw["g_ffn"], lw["w_ff1"], lw["w_ff2"], gf, tm=min(n, 512), tf=512,
             final_norm=final_gain is not None)
    return h, ret_new, conv_new, hgrn_new, ckv, kpe


def kernel(x_prompt, x_sample, cache_mla_latent, cache_mla_rope, page_table,
           state_retention, state_conv, state_hgrn,
           norm_mix, w_in, conv_w, mla_q_norm, mla_w_uq, mla_kv_norm, mla_w_uk, mla_w_uv,
           hgrn_lb_logits, hgrn_out_norm, w_branch_out, w_out, norm_ffn, w_ff1, w_ff2, norm_final):
    bp, tp, d = x_prompt.shape
    bs, ts, _ = x_sample.shape
    depth = w_in.shape[0]
    past_len = page_table.shape[1] * PAGE_SIZE
    tabs_p = _rope_tables(jnp.arange(tp, dtype=jnp.int32))
    tabs_s = _rope_tables(past_len + jnp.arange(ts, dtype=jnp.int32))
    lb_p = jax.nn.softmax(hgrn_lb_logits.astype(F32), axis=0)
    lower_bounds = jnp.cumsum(lb_p, axis=0) - lb_p[0:1]
    gfin = norm_final.reshape(1, d)
    rope_pool_t = jnp.swapaxes(cache_mla_rope, 2, 3)

    hp = x_prompt.reshape(bp * tp, d)
    hs = x_sample.reshape(bs * ts, d)
    zeros_ret = jnp.zeros((bp, RET_HEADS, RET_DK, RET_DV), F32)
    zeros_conv = jnp.zeros((bp, CONV_K - 1, CONV_WIDTH), F32)
    zeros_hgrn = jnp.zeros((bp, HGRN_HEADS, HGRN_DK, HGRN_DV), F32)
    outs = {k: [] for k in ("lat_p", "rope_p", "lat_s", "rope_s", "ret_p", "ret_s",
                            "conv_p", "conv_s", "hg_p", "hg_s")}
    for l in range(depth):
        lw = _pack_layer(w_in[l], conv_w[l], mla_q_norm[l], mla_w_uq[l], mla_kv_norm[l], mla_w_uk[l],
                         mla_w_uv[l], hgrn_out_norm[l], w_branch_out[l], w_out[l], norm_mix[l],
                         norm_ffn[l], w_ff1[l], w_ff2[l])
        lb = lower_bounds[l].reshape(1, HGRN_W)
        last = gfin if l == depth - 1 else None

        def attend_prompt(proj, lw, tabs, tl, odt):
            q, ckv, kpe, kc, v = _mla_prep(proj, lw, tabs, n=bp * tp, t=tp, tm=tl["prep"], with_kv=True)
            o = _flash(q, kc, v, b=bp, t=tp, tq=tl["tq"], tk=tl["tk"], odt=odt)
            return o, ckv, kpe

        def attend_sample(proj, lw, tabs, tl, odt, layer=l):
            q, ckv, kpe = _mla_prep(proj, lw, tabs, n=bs * ts, t=ts, tm=tl["prep"], with_kv=False)
            o = _paged_attention(page_table, q.reshape(bs, ts, -1), ckv.reshape(bs, ts, -1),
                                 kpe.reshape(bs, ts, -1), lw["wabs"], lw["wuv"],
                                 cache_mla_latent, rope_pool_t, layer=layer,
                                 pg=math.gcd(16, page_table.shape[1]), nb=math.gcd(2, bs))
            return o.reshape(bs * ts, MLA_W), ckv, kpe

        hp, r, c, g, ckv, kpe = _trunk_layer(hp, tabs_p, zeros_ret, zeros_conv, zeros_hgrn, lb, lw,
                                             b=bp, t=tp, attend=attend_prompt, final_gain=last)
        outs["ret_p"].append(r); outs["conv_p"].append(c); outs["hg_p"].append(g)
        outs["lat_p"].append(ckv.reshape(bp, tp, -1)); outs["rope_p"].append(kpe.reshape(bp, tp, -1))
        hs, r, c, g, ckv, kpe = _trunk_layer(hs, tabs_s, state_retention[l], state_conv[l], state_hgrn[l],
                                             lb, lw, b=bs, t=ts, attend=attend_sample, final_gain=last)
        outs["ret_s"].append(r); outs["conv_s"].append(c); outs["hg_s"].append(g)
        outs["lat_s"].append(ckv.reshape(bs, ts, -1)); outs["rope_s"].append(kpe.reshape(bs, ts, -1))

    st = lambda k: jnp.stack(outs[k])
    return (hp.reshape(bp, tp, d), hs.reshape(bs, ts, d),
            st("lat_p"), st("rope_p"), st("lat_s"), st("rope_s"),
            st("ret_p"), st("ret_s"), st("conv_p"), st("conv_s"), st("hg_p"), st("hg_s"))
```

```python
import functools
import math

import numpy as np
import jax
import jax.numpy as jnp
from jax import lax
from jax.experimental import pallas as pl
from jax.experimental.pallas import tpu as pltpu

F32 = jnp.float32
BF16 = jnp.bfloat16

D_MODEL = 1024
PAGE_SIZE = 128
RET_HEADS, RET_DK, RET_DV, RET_CHUNK = 4, 64, 128, 128
CONV_WIDTH, CONV_K = 512, 3
MLA_HEADS, MLA_Q_LORA, MLA_KV_LORA, MLA_NOPE, MLA_ROPE, MLA_V = 8, 384, 256, 64, 32, 64
MLA_SCALE = (MLA_NOPE + MLA_ROPE) ** -0.5
HGRN_HEADS, HGRN_DK, HGRN_DV = 4, 128, 128
D_FF = 4 * D_MODEL
ROPE_THETA = 10000.0
NORM_EPS = 1e-6
N_BRANCH = 4
NEG_BIG = -1e30
LOG2E = math.log2(math.e)

RET_W = RET_HEADS * RET_DV
MLA_W = MLA_HEADS * MLA_V
HGRN_W = HGRN_HEADS * HGRN_DV
IN_SIZES = (
    RET_HEADS * RET_DK, RET_HEADS * RET_DK, RET_W, RET_W,
    CONV_WIDTH, CONV_WIDTH, CONV_WIDTH,
    MLA_Q_LORA, MLA_KV_LORA, MLA_ROPE,
    HGRN_HEADS * HGRN_DK, HGRN_HEADS * HGRN_DK, HGRN_W, HGRN_W,
    N_BRANCH * D_MODEL,
)

LANES = 128
SUBLANES = 8
VMEM_LIMIT_BYTES = 56 * 1024 * 1024

IN_TILE = 768
GATE_W = N_BRANCH * D_MODEL
RV_OFF = GATE_W
PROJ_A_W = GATE_W + RET_W
HG_BLOCK_W = 4 * HGRN_W
RET_QKG_W = 1024
RET_QKG_OFF = HG_BLOCK_W
CONV_BLOCK_W = 3 * CONV_WIDTH
CONV_OFF = RET_QKG_OFF + RET_QKG_W
MLA_BLOCK_W = 768
MLA_OFF = CONV_OFF + CONV_BLOCK_W
PROJ_B_W = MLA_OFF + MLA_BLOCK_W
MLA_HEAD_PAD = 128
HGRN_SUB = 16
HGRN_CHUNK = 64
HGRN_SAFE_LOG_DECAY = 160.0
KEY_PAD = 128


def _cparams(sem):
    return pltpu.CompilerParams(dimension_semantics=sem, vmem_limit_bytes=VMEM_LIMIT_BYTES)


def _rms(x, g):
    return x * lax.rsqrt(jnp.mean(x * x, axis=-1, keepdims=True) + NORM_EPS) * g


def _silu(x):
    return x * jax.nn.sigmoid(x)


def _dot(a, b):
    return jnp.dot(a, b, preferred_element_type=F32)


def _dot_nt(a, b):
    return lax.dot_general(a, b, (((1,), (1,)), ((), ())), preferred_element_type=F32)


def _dot_tn(a, b):
    return lax.dot_general(a, b, (((0,), (0,)), ((), ())), preferred_element_type=F32)


def _inproj_kernel(x_ref, g_ref, w_ref, oa_ref, ob_ref, xn_ref, *, na):
    j = pl.program_id(1)

    @pl.when(j == 0)
    def _():
        xn_ref[...] = _rms(x_ref[...], g_ref[...]).astype(BF16)

    @pl.when(j < na)
    def _():
        oa_ref[...] = _dot(xn_ref[...], w_ref[...]).astype(oa_ref.dtype)

    @pl.when(j >= na)
    def _():
        ob_ref[...] = _dot(xn_ref[...], w_ref[...])


def _inproj(x, g, w, *, tm, adt):
    n, d = x.shape
    tn = IN_TILE
    na = PROJ_A_W // tn
    nb = PROJ_B_W // tn
    return pl.pallas_call(
        functools.partial(_inproj_kernel, na=na),
        out_shape=(jax.ShapeDtypeStruct((n, PROJ_A_W), adt), jax.ShapeDtypeStruct((n, PROJ_B_W), F32)),
        grid=(n // tm, na + nb),
        in_specs=[
            pl.BlockSpec((tm, d), lambda i, j: (i, 0)),
            pl.BlockSpec((1, d), lambda i, j: (0, 0)),
            pl.BlockSpec((d, tn), lambda i, j: (0, j)),
        ],
        out_specs=(pl.BlockSpec((tm, tn), lambda i, j: (i, jnp.minimum(j, na - 1))),
                   pl.BlockSpec((tm, tn), lambda i, j: (i, jnp.maximum(j - na, 0)))),
        scratch_shapes=[pltpu.VMEM((tm, d), BF16)],
        compiler_params=_cparams(("parallel", "arbitrary")),
        name="in_proj",
    )(x, g, w)


def _ret_kernel(x_ref, v_ref, cos_ref, sin_ref, decay_ref, qdec_ref, kdec_ref, s0_ref,
                o_ref, sout_ref, s_ref, *, nt, cdec, c):
    t = pl.program_id(1)
    hw = RET_HEADS * RET_DK // 2
    per = RET_DK // 2

    @pl.when(t == 0)
    def _():
        s_ref[...] = jnp.zeros(s_ref.shape, F32)
        for h in range(RET_HEADS):
            s_ref[h, h * per:(h + 1) * per, :] = s0_ref[0, h, 0:per, :]
            s_ref[h, hw + h * per:hw + (h + 1) * per, :] = s0_ref[0, h, per:2 * per, :]

    lane = lax.broadcasted_iota(jnp.int32, (1, 2 * hw), 1)
    head_of_lane = (lane % hw) // per
    for sub in range(x_ref.shape[0] // c):
        r0 = sub * c
        x = x_ref[r0:r0 + c, :]
        cos = cos_ref[r0:r0 + c, :]
        sin = sin_ref[r0:r0 + c, :]
        q1, q2 = x[:, 0:hw], x[:, hw:2 * hw]
        k1, k2 = x[:, 2 * hw:3 * hw], x[:, 3 * hw:4 * hw]
        qr = jnp.concatenate([q1 * cos - q2 * sin, q1 * sin + q2 * cos], axis=1)
        kr = jnp.concatenate([k1 * cos - k2 * sin, k1 * sin + k2 * cos], axis=1) * (RET_DK ** -0.5)
        v = v_ref[r0:r0 + c, :]
        g = x[:, 4 * hw:4 * hw + RET_W]
        if c < KEY_PAD:
            kr = jnp.concatenate([kr, jnp.zeros((KEY_PAD - c, kr.shape[1]), F32)], axis=0)
            v = jnp.concatenate([v, jnp.zeros((KEY_PAD - c, v.shape[1]), v.dtype)], axis=0)
        kt = kr.T
        ktb = kt.astype(BF16)
        for h in range(RET_HEADS):
            qm = jnp.where(head_of_lane == h, qr, 0.0).astype(BF16)
            vh = v[:, h * RET_DV:(h + 1) * RET_DV].astype(BF16)
            sh = s_ref[h]
            att = _dot(qm, ktb) * decay_ref[h]
            o = _dot(att.astype(BF16), vh) + _dot(qm, sh.astype(BF16)) * qdec_ref[h]
            s_ref[h] = sh * cdec[h] + _dot((kt * kdec_ref[h]).astype(BF16), vh)
            o = o * lax.rsqrt(jnp.mean(o * o, axis=-1, keepdims=True) + NORM_EPS)
            gh = g[:, h * RET_DV:(h + 1) * RET_DV]
            o_ref[r0:r0 + c, h * RET_DV:(h + 1) * RET_DV] = (o * _silu(gh)).astype(o_ref.dtype)

    @pl.when(t == nt - 1)
    def _():
        for h in range(RET_HEADS):
            sout_ref[0, h, 0:per, :] = s_ref[h, h * per:(h + 1) * per, :]
            sout_ref[0, h, per:2 * per, :] = s_ref[h, hw + h * per:hw + (h + 1) * per, :]


def _retention(proj_a, proj_b, cos, sin, state, *, b, t, odt):
    c = math.gcd(t, RET_CHUNK)
    rows = c * math.gcd(t // c, 2)
    nt = t // rows
    ck = max(c, KEY_PAD)
    hw = RET_HEADS * RET_DK // 2
    hs = np.arange(RET_HEADS, dtype=np.float64)
    log_gamma = np.log1p(-np.exp2(-5.0 - hs))
    j = np.arange(c, dtype=np.float64)
    rel = j[:, None] - j[None, :]
    decay = np.where(rel >= 0, np.exp(log_gamma[:, None, None] * np.where(rel >= 0, rel, 0.0)), 0.0)
    decay_p = np.zeros((RET_HEADS, c, ck))
    decay_p[:, :, :c] = decay
    qdec = np.repeat(np.exp(log_gamma[:, None] * (j + 1.0))[:, :, None], RET_DV, axis=2)
    kdec = np.zeros((RET_HEADS, 1, ck))
    kdec[:, 0, :c] = np.exp(log_gamma[:, None] * (c - 1.0 - j))
    cdec = tuple(float(np.float32(np.exp(lg * c))) for lg in log_gamma)
    kern = functools.partial(_ret_kernel, nt=nt, cdec=cdec, c=c)
    return pl.pallas_call(
        kern,
        out_shape=(jax.ShapeDtypeStruct((b * t, RET_W), odt),
                   jax.ShapeDtypeStruct((b, RET_HEADS, RET_DK, RET_DV), F32)),
        grid=(b, nt),
        in_specs=[
            pl.BlockSpec((rows, RET_QKG_W), lambda i, k: (i * nt + k, RET_QKG_OFF // RET_QKG_W)),
            pl.BlockSpec((rows, RET_W), lambda i, k: (i * nt + k, RV_OFF // RET_W)),
            pl.BlockSpec((rows, hw), lambda i, k: (k, 0)),
            pl.BlockSpec((rows, hw), lambda i, k: (k, 0)),
            pl.BlockSpec((RET_HEADS, c, ck), lambda i, k: (0, 0, 0)),
            pl.BlockSpec((RET_HEADS, c, RET_DV), lambda i, k: (0, 0, 0)),
            pl.BlockSpec((RET_HEADS, 1, ck), lambda i, k: (0, 0, 0)),
            pl.BlockSpec((1, RET_HEADS, RET_DK, RET_DV), lambda i, k: (i, 0, 0, 0)),
        ],
        out_specs=(pl.BlockSpec((rows, RET_W), lambda i, k: (i * nt + k, 0)),
                   pl.BlockSpec((1, RET_HEADS, RET_DK, RET_DV), lambda i, k: (i, 0, 0, 0))),
        scratch_shapes=[pltpu.VMEM((RET_HEADS, RET_HEADS * RET_DK, RET_DV), F32)],
        compiler_params=_cparams(("parallel", "arbitrary")),
        name="retention",
    )(proj_b, proj_a, cos, sin, jnp.asarray(decay_p, F32), jnp.asarray(qdec, F32), jnp.asarray(kdec, F32), state)


def _conv_kernel(x_ref, w_ref, buf_ref, o_ref, bout_ref, carry_ref, *, nt):
    t = pl.program_id(1)
    w = CONV_WIDTH

    @pl.when(t == 0)
    def _():
        carry_ref[...] = jnp.zeros(carry_ref.shape, F32)
        carry_ref[SUBLANES - 2:SUBLANES, :] = buf_ref[0]

    x = x_ref[...]
    rows = x.shape[0]
    gate_b, gate_c, xin = x[:, 0:w], x[:, w:2 * w], x[:, 2 * w:3 * w]
    u = gate_c * xin
    row = lax.broadcasted_iota(jnp.int32, (rows, 1), 0)
    p1 = carry_ref[SUBLANES - 1:SUBLANES, :]
    p2 = carry_ref[SUBLANES - 2:SUBLANES - 1, :]
    u1 = jnp.where(row == 0, p1, pltpu.roll(u, 1, 0))
    u2 = jnp.where(row == 0, p2, jnp.where(row == 1, p1, pltpu.roll(u, 2, 0)))
    y = u2 * w_ref[0:1, :] + u1 * w_ref[1:2, :] + u * w_ref[2:3, :]
    o_ref[...] = (gate_b * y).astype(o_ref.dtype)
    carry_ref[...] = u[rows - SUBLANES:rows, :]

    @pl.when(t == nt - 1)
    def _():
        bout_ref[0] = u[rows - 2:rows, :]


def _short_conv(proj, w, buf, *, b, t, rows, odt):
    nt = t // rows
    kern = functools.partial(_conv_kernel, nt=nt)
    return pl.pallas_call(
        kern,
        out_shape=(jax.ShapeDtypeStruct((b * t, CONV_WIDTH), odt),
                   jax.ShapeDtypeStruct((b, CONV_K - 1, CONV_WIDTH), F32)),
        grid=(b, nt),
        in_specs=[
            pl.BlockSpec((rows, CONV_BLOCK_W), lambda i, k: (i * nt + k, CONV_OFF // CONV_BLOCK_W)),
            pl.BlockSpec((CONV_K, CONV_WIDTH), lambda i, k: (0, 0)),
            pl.BlockSpec((1, CONV_K - 1, CONV_WIDTH), lambda i, k: (i, 0, 0)),
        ],
        out_specs=(pl.BlockSpec((rows, CONV_WIDTH), lambda i, k: (i * nt + k, 0)),
                   pl.BlockSpec((1, CONV_K - 1, CONV_WIDTH), lambda i, k: (i, 0, 0))),
        scratch_shapes=[pltpu.VMEM((SUBLANES, CONV_WIDTH), F32)],
        compiler_params=_cparams(("parallel", "arbitrary")),
        name="short_conv",
    )(proj, w, buf)


def _split3(x):
    hi = x.astype(BF16)
    r = x - hi.astype(F32)
    mid = r.astype(BF16)
    lo = (r - mid.astype(F32)).astype(BF16)
    return hi, mid, lo


def _hgrn_intra_exact(gcum, qh, kk, hv, g_ref, k_ref, oi_ref):
    ell = gcum.shape[0]
    dk = HGRN_DK
    g_ref[...] = gcum
    k_ref[...] = kk
    nsub = ell // HGRN_SUB
    lane = lax.broadcasted_iota(jnp.int32, (1, KEY_PAD), 1)
    key_row = lax.broadcasted_iota(jnp.int32, (ell, 1), 0)
    sub_row = lax.broadcasted_iota(jnp.int32, (HGRN_SUB, 1), 0)
    a_blocks = [[None] * nsub for _ in range(HGRN_HEADS)]
    for j in range(nsub):
        r0 = j * HGRN_SUB
        gq = gcum[r0:r0 + HGRN_SUB, :]
        qq = qh[r0:r0 + HGRN_SUB, :]

        def diag_body(s, carry, r0=r0, gq=gq, qq=qq):
            gs = g_ref[pl.ds(r0 + s, 1), :]
            ks = k_ref[pl.ds(r0 + s, 1), :]
            valid = sub_row >= s
            p = jnp.where(valid, jnp.exp(jnp.minimum(gq - gs, 0.0)) * qq * ks, 0.0)
            out = []
            for h in range(HGRN_HEADS):
                a = jnp.sum(p[:, h * dk:(h + 1) * dk], axis=-1, keepdims=True)
                out.append(carry[h] + jnp.where(lane == r0 + s, a, 0.0))
            return tuple(out)

        init = tuple(jnp.zeros((HGRN_SUB, KEY_PAD), F32) for _ in range(HGRN_HEADS))
        diag = lax.fori_loop(0, HGRN_SUB, diag_body, init)
        if j > 0:
            gb = gcum[r0 - 1:r0, :]
            q_t = (qq * jnp.exp(gq - gb)).astype(BF16)
            k_t = jnp.where(key_row < r0, kk * jnp.exp(jnp.minimum(gb - gcum, 0.0)), 0.0)
            if ell < KEY_PAD:
                k_t = jnp.concatenate([k_t, jnp.zeros((KEY_PAD - ell, k_t.shape[1]), F32)], axis=0)
            k_t = k_t.astype(BF16)
        for h in range(HGRN_HEADS):
            blk = diag[h]
            if j > 0:
                blk = blk + _dot_nt(q_t[:, h * dk:(h + 1) * dk], k_t[:, h * dk:(h + 1) * dk])
            a_blocks[h][j] = blk
    for h in range(HGRN_HEADS):
        a_h = a_blocks[h][0] if nsub == 1 else jnp.concatenate(a_blocks[h], axis=0)
        vh = hv[:, h * HGRN_DV:(h + 1) * HGRN_DV]
        if ell < KEY_PAD:
            vh = jnp.concatenate([vh, jnp.zeros((KEY_PAD - ell, HGRN_DV), F32)], axis=0)
        oi_ref[h] = _dot(a_h.astype(BF16), vh.astype(BF16))


def _hgrn_intra_fast(gcum, g_end, qh, kk, hv, oi_ref):
    ell = gcum.shape[0]
    dk = HGRN_DK
    gm = 0.5 * g_end
    q_t = (qh * jnp.exp(gcum - gm)).astype(BF16)
    k_t = (kk * jnp.exp(gm - gcum)).astype(BF16)
    causal = lax.broadcasted_iota(jnp.int32, (ell, ell), 0) >= lax.broadcasted_iota(jnp.int32, (ell, ell), 1)
    for h in range(HGRN_HEADS):
        a = jnp.where(causal, _dot_nt(q_t[:, h * dk:(h + 1) * dk], k_t[:, h * dk:(h + 1) * dk]), 0.0)
        oi_ref[h] = _dot(a.astype(BF16), hv[:, h * HGRN_DV:(h + 1) * HGRN_DV].astype(BF16))


def _hgrn_kernel(x_ref, lb_ref, gn_ref, s0_ref, o_ref, sout_ref,
                 st_ref, g_ref, k_ref, oi_ref, *, nt, chunk):
    t = pl.program_id(1)
    w = HGRN_W
    dk = HGRN_DK

    @pl.when(t == 0)
    def _():
        for h in range(HGRN_HEADS):
            st_ref[h] = s0_ref[0, h].T

    rows = x_ref.shape[0]
    lb = lb_ref[...]
    gn = gn_ref[...]
    ri = lax.broadcasted_iota(jnp.int32, (chunk, chunk), 0)
    ci = lax.broadcasted_iota(jnp.int32, (chunk, chunk), 1)
    tri = jnp.where(ri >= ci, 1.0, 0.0).astype(BF16)
    parts = []
    for c in range(max(rows // chunk, 1)):
        if rows >= chunk:
            x = x_ref[c * chunk:(c + 1) * chunk, :]
            live = chunk
        else:
            x = jnp.concatenate([x_ref[...], jnp.zeros((chunk - rows, x_ref.shape[1]), F32)], axis=0)
            live = rows
        hq, hf, hv, hg = x[:, 0:w], x[:, w:2 * w], x[:, 2 * w:3 * w], x[:, 3 * w:4 * w]
        f_gate = lb + (1.0 - lb) * jax.nn.sigmoid(hf)
        log_f = jnp.log(f_gate)
        kk = 1.0 - f_gate
        if live < chunk:
            ok = lax.broadcasted_iota(jnp.int32, (chunk, 1), 0) < live
            log_f = jnp.where(ok, log_f, 0.0)
            kk = jnp.where(ok, kk, 0.0)
        qh = _silu(hq)
        p_hi, p_mid, p_lo = _split3(log_f)
        gcum = _dot(tri, p_hi) + _dot(tri, p_mid) + _dot(tri, p_lo)
        parts.append((gcum, gcum[chunk - 1:chunk, :], qh, kk, hv, hg, live))

    g_min = parts[0][1]
    for prt in parts[1:]:
        g_min = jnp.minimum(g_min, prt[1])
    safe = jnp.min(g_min) > -HGRN_SAFE_LOG_DECAY

    @pl.when(safe)
    def _():
        for c, (gcum, g_end, qh, kk, hv, _, _) in enumerate(parts):
            _hgrn_intra_fast(gcum, g_end, qh, kk, hv, oi_ref.at[c])

    @pl.when(jnp.logical_not(safe))
    def _():
        for c, (gcum, g_end, qh, kk, hv, _, _) in enumerate(parts):
            _hgrn_intra_exact(gcum, qh, kk, hv, g_ref, k_ref, oi_ref.at[c])

    for c, (gcum, g_end, qh, kk, hv, hg, live) in enumerate(parts):
        q_in = (qh * jnp.exp(gcum)).astype(BF16)
        k_out = (kk * jnp.exp(g_end - gcum)).astype(BF16)
        e_end = jnp.exp(g_end)
        for h in range(HGRN_HEADS):
            sl = slice(h * dk, (h + 1) * dk)
            st = st_ref[h]
            o = oi_ref[c, h] + _dot_nt(q_in[:, sl], st.astype(BF16))
            st_ref[h] = st * e_end[:, sl] + _dot_tn(hv[:, h * HGRN_DV:(h + 1) * HGRN_DV].astype(BF16), k_out[:, sl])
            o = _rms(o, gn) * _silu(hg[:, h * HGRN_DV:(h + 1) * HGRN_DV])
            o_ref[c * chunk:c * chunk + live, h * HGRN_DV:(h + 1) * HGRN_DV] = o[0:live, :].astype(o_ref.dtype)

    @pl.when(t == nt - 1)
    def _():
        for h in range(HGRN_HEADS):
            sout_ref[0, h] = st_ref[h].T


def _hgrn(proj, lb, gnorm, state, *, b, t, rows, odt):
    nt = t // rows
    chunk = max(min(rows, HGRN_CHUNK), HGRN_SUB)
    nchunk = max(rows // chunk, 1)
    kern = functools.partial(_hgrn_kernel, nt=nt, chunk=chunk)
    return pl.pallas_call(
        kern,
        out_shape=(jax.ShapeDtypeStruct((b * t, HGRN_W), odt),
                   jax.ShapeDtypeStruct((b, HGRN_HEADS, HGRN_DK, HGRN_DV), F32)),
        grid=(b, nt),
        in_specs=[
            pl.BlockSpec((rows, HG_BLOCK_W), lambda i, k: (i * nt + k, 0)),
            pl.BlockSpec((1, HGRN_W), lambda i, k: (0, 0)),
            pl.BlockSpec((1, HGRN_DV), lambda i, k: (0, 0)),
            pl.BlockSpec((1, HGRN_HEADS, HGRN_DK, HGRN_DV), lambda i, k: (i, 0, 0, 0)),
        ],
        out_specs=(pl.BlockSpec((rows, HGRN_W), lambda i, k: (i * nt + k, 0)),
                   pl.BlockSpec((1, HGRN_HEADS, HGRN_DK, HGRN_DV), lambda i, k: (i, 0, 0, 0))),
        scratch_shapes=[pltpu.VMEM((HGRN_HEADS, HGRN_DV, HGRN_DK), F32),
                        pltpu.VMEM((chunk, HGRN_W), F32),
                        pltpu.VMEM((chunk, HGRN_W), F32),
                        pltpu.VMEM((nchunk, HGRN_HEADS, chunk, HGRN_DV), F32)],
        compiler_params=_cparams(("parallel", "arbitrary")),
        name="hgrn2",
    )(proj, lb, gnorm, state)


def _mla_prep_kernel(x_ref, qn_ref, kvn_ref, wqn_ref, wqp_ref, wqr_ref, eq_ref,
                     cq_ref, sq_ref, ck_ref, sk_ref, *rest, with_kv):
    if with_kv:
        wuk_ref, ek_ref, wuv_ref, q_ref, lat_ref, kpe_ref, kc_ref, v_ref = rest
    else:
        q_ref, lat_ref, kpe_ref = rest
    x = x_ref[...]
    mcq = x[:, 0:MLA_Q_LORA]
    mckv = x[:, MLA_Q_LORA:MLA_Q_LORA + MLA_KV_LORA]
    o = MLA_Q_LORA + MLA_KV_LORA
    mkr = x[:, o:o + MLA_ROPE]
    mkr_rot = x[:, o + MLA_ROPE:o + 2 * MLA_ROPE]
    xq = _rms(mcq, qn_ref[...]).astype(BF16)
    q_pe = _dot(xq, wqp_ref[...]) * cq_ref[...] + _dot(xq, wqr_ref[...]) * sq_ref[...]
    q_cat = _dot(xq, wqn_ref[...]) + _dot(q_pe.astype(BF16), eq_ref[...])
    q_ref[...] = (q_cat * (MLA_SCALE * LOG2E)).astype(q_ref.dtype)
    ckv = _rms(mckv, kvn_ref[...])
    lat_ref[...] = ckv
    kpe = mkr * ck_ref[...] + mkr_rot * sk_ref[...]
    kpe_ref[...] = kpe
    if with_kv:
        cb = ckv.astype(BF16)
        kc_ref[...] = (_dot(cb, wuk_ref[...]) + _dot(kpe.astype(BF16), ek_ref[...])).astype(BF16)
        v_ref[0] = _dot_nt(wuv_ref[...], cb).astype(BF16)


def _mla_prep(proj, lw, tabs, *, n, t, tm, with_kv):
    rows = min(tm, t)
    nt = t // rows
    hp = MLA_HEADS * MLA_HEAD_PAD
    qp_w = MLA_HEADS * MLA_ROPE
    full = lambda r, c: pl.BlockSpec((r, c), lambda i: (0, 0))
    tab = lambda c: pl.BlockSpec((rows, c), lambda i: (i % nt, 0))
    in_specs = [
        pl.BlockSpec((rows, MLA_BLOCK_W), lambda i: (i, MLA_OFF // MLA_BLOCK_W)),
        full(1, MLA_Q_LORA), full(1, MLA_KV_LORA),
        full(MLA_Q_LORA, hp), full(MLA_Q_LORA, qp_w), full(MLA_Q_LORA, qp_w), full(qp_w, hp),
        tab(qp_w), tab(qp_w), tab(MLA_ROPE), tab(MLA_ROPE),
    ]
    args = [proj, lw["q_norm"], lw["kv_norm"], lw["wq_nope"], lw["wq_pe"], lw["wq_rot"], lw["e_q"],
            tabs["cos_q"], tabs["sin_q"], tabs["cos_k"], tabs["sin_k"]]
    q_dtype = BF16 if with_kv else F32
    out_shape = [jax.ShapeDtypeStruct((n, hp), q_dtype),
                 jax.ShapeDtypeStruct((n, MLA_KV_LORA), F32),
                 jax.ShapeDtypeStruct((n, MLA_ROPE), F32)]
    out_specs = [pl.BlockSpec((rows, hp), lambda i: (i, 0)),
                 pl.BlockSpec((rows, MLA_KV_LORA), lambda i: (i, 0)),
                 pl.BlockSpec((rows, MLA_ROPE), lambda i: (i, 0))]
    if with_kv:
        in_specs += [full(MLA_KV_LORA, hp), full(MLA_ROPE, hp), full(MLA_W, MLA_KV_LORA)]
        args += [lw["wuk_pad"], lw["e_k"], lw["wuv_t"]]
        out_shape += [jax.ShapeDtypeStruct((n, hp), BF16), jax.ShapeDtypeStruct((n // t, MLA_W, t), BF16)]
        out_specs += [pl.BlockSpec((rows, hp), lambda i: (i, 0)),
                      pl.BlockSpec((1, MLA_W, rows), lambda i: (i // nt, 0, i % nt))]
    return pl.pallas_call(
        functools.partial(_mla_prep_kernel, with_kv=with_kv),
        out_shape=tuple(out_shape),
        grid=(n // rows,),
        in_specs=in_specs,
        out_specs=tuple(out_specs),
        compiler_params=_cparams(("parallel",)),
        name="mla_prep",
    )(*args)


def _flash_kernel(q_ref, k_ref, vt_ref, o_ref, m_ref, l_ref, acc_ref, *, tq, tk):
    qi = pl.program_id(1)
    kj = pl.program_id(2)
    hp = MLA_HEAD_PAD

    @pl.when(kj == 0)
    def _():
        m_ref[...] = jnp.full(m_ref.shape, NEG_BIG, F32)
        l_ref[...] = jnp.zeros(l_ref.shape, F32)
        acc_ref[...] = jnp.zeros(acc_ref.shape, F32)

    def step(mask):
        for h in range(MLA_HEADS):
            q = q_ref[:, h * hp:(h + 1) * hp]
            k = k_ref[:, h * hp:(h + 1) * hp]
            s = _dot_nt(k, q)
            if mask is not None:
                s = jnp.where(mask, s, NEG_BIG)
            m_old = m_ref[h]
            m_new = jnp.maximum(m_old, jnp.max(s, axis=0, keepdims=True))
            alpha = jnp.exp2(m_old - m_new)
            p = jnp.exp2(s - m_new)
            l_ref[h] = l_ref[h] * alpha + jnp.sum(p, axis=0, keepdims=True)
            vt = vt_ref[0, h * MLA_V:(h + 1) * MLA_V, :]
            acc_ref[h] = acc_ref[h] * alpha + _dot(vt, p.astype(BF16))
            m_ref[h] = m_new

    @pl.when(kj < qi)
    def _():
        step(None)

    @pl.when(kj == qi)
    def _():
        step(lax.broadcasted_iota(jnp.int32, (tk, tq), 0) <= lax.broadcasted_iota(jnp.int32, (tk, tq), 1))
        for h in range(MLA_HEADS):
            acc_ref[h] = acc_ref[h] / l_ref[h]
        o_ref[...] = acc_ref[...].reshape(MLA_W, tq).T.astype(o_ref.dtype)


def _flash(q, k, vt, *, b, t, tq, tk, odt):
    nq, nk = t // tq, t // tk
    hp = MLA_HEADS * MLA_HEAD_PAD
    kern = functools.partial(_flash_kernel, tq=tq, tk=tk)
    return pl.pallas_call(
        kern,
        out_shape=jax.ShapeDtypeStruct((b * t, MLA_W), odt),
        grid=(b, nq, nk),
        in_specs=[
            pl.BlockSpec((tq, hp), lambda i, a, c: (i * nq + a, 0)),
            pl.BlockSpec((tk, hp), lambda i, a, c: (i * nk + jnp.minimum(c, a), 0)),
            pl.BlockSpec((1, MLA_W, tk), lambda i, a, c: (i, 0, jnp.minimum(c, a))),
        ],
        out_specs=pl.BlockSpec((tq, MLA_W), lambda i, a, c: (i * nq + a, 0)),
        scratch_shapes=[pltpu.VMEM((MLA_HEADS, 1, tq), F32),
                        pltpu.VMEM((MLA_HEADS, 1, tq), F32),
                        pltpu.VMEM((MLA_HEADS, MLA_V, tq), F32)],
        compiler_params=_cparams(("parallel", "parallel", "arbitrary")),
        name="mla_flash",
    )(q, k, vt)


def _paged_kernel(pt_ref, q_ref, lat_new_ref, kpe_new_ref, wabs_ref, wuv_ref, *rest,
                  ng, pg, t, nb):
    npg = nb * pg
    lat_refs = rest[0:npg]
    pe_refs = rest[npg:2 * npg]
    o_ref = rest[2 * npg]
    ql_ref, qp_ref, m_ref, l_ref, acc_ref, c_ref, rt_ref = rest[2 * npg + 1:]
    g = pl.program_id(1)
    hp = MLA_HEAD_PAD
    rows = MLA_HEADS * t

    @pl.when(g == 0)
    def _():
        lane_head = lax.broadcasted_iota(jnp.int32, (1, MLA_HEADS * hp), 1) // hp
        for e in range(nb):
            q = q_ref[e]
            q_exp = jnp.concatenate(
                [jnp.where(lane_head == h, q, 0.0) for h in range(MLA_HEADS)], axis=0).astype(BF16)
            q_abs = _dot(q_exp, wabs_ref[...])
            ql_ref[e] = q_abs[:, 0:MLA_KV_LORA].astype(BF16)
            qp_ref[e] = q_abs[:, MLA_KV_LORA:MLA_KV_LORA + LANES].astype(BF16)
        m_ref[...] = jnp.full(m_ref.shape, NEG_BIG, F32)
        l_ref[...] = jnp.zeros(l_ref.shape, F32)
        acc_ref[...] = jnp.zeros(acc_ref.shape, F32)

    def update(e, s, c):
        m_old = m_ref[e]
        m_new = jnp.maximum(m_old, jnp.max(s, axis=-1, keepdims=True))
        alpha = jnp.exp2(m_old - m_new)
        p = jnp.exp2(s - m_new)
        l_ref[e] = l_ref[e] * alpha + jnp.sum(p, axis=-1, keepdims=True)
        acc_ref[e] = acc_ref[e] * alpha + _dot(p.astype(BF16), c)
        m_ref[e] = m_new

    for e in range(nb):
        for i in range(pg):
            c_ref[e, i * PAGE_SIZE:(i + 1) * PAGE_SIZE, :] = lat_refs[e * pg + i][0, 0].astype(BF16)
            rt_ref[e, :, i * PAGE_SIZE:(i + 1) * PAGE_SIZE] = pe_refs[e * pg + i][0, 0].astype(BF16)
    for e in range(nb):
        c_all = c_ref[e]
        update(e, _dot_nt(ql_ref[e], c_all) + _dot(qp_ref[e, :, 0:MLA_ROPE], rt_ref[e]), c_all)

    @pl.when(g == ng - 1)
    def _():
        qtok = lax.broadcasted_iota(jnp.int32, (rows, KEY_PAD), 0) % t
        kidx = lax.broadcasted_iota(jnp.int32, (rows, KEY_PAD), 1)
        lane_head = lax.broadcasted_iota(jnp.int32, (1, MLA_W), 1) // MLA_V
        for e in range(nb):
            cn = jnp.concatenate([lat_new_ref[e], jnp.zeros((KEY_PAD - t, MLA_KV_LORA), F32)], axis=0).astype(BF16)
            rn = jnp.concatenate([kpe_new_ref[e], jnp.zeros((KEY_PAD - t, MLA_ROPE), F32)], axis=0).astype(BF16)
            s = _dot_nt(ql_ref[e], cn) + _dot_nt(qp_ref[e, :, 0:MLA_ROPE], rn)
            update(e, jnp.where(kidx <= qtok, s, NEG_BIG), cn)
            o_lat = acc_ref[e] / l_ref[e]
            res = _dot(o_lat.astype(BF16), wuv_ref[...])
            out = jnp.zeros((t, MLA_W), F32)
            for h in range(MLA_HEADS):
                out = out + jnp.where(lane_head == h, res[h * t:(h + 1) * t, :], 0.0)
            o_ref[e] = out


def _paged_attention(page_table, q, lat_new, kpe_new, wabs, wuv, lat_pool, pe_pool, *, layer, pg, nb):
    b, t, hp = q.shape
    n_pages = page_table.shape[1]
    ng = n_pages // pg
    rows = MLA_HEADS * t

    def page_spec(r, c, e, i):
        return pl.BlockSpec((1, 1, r, c), lambda bi, g, pt: (layer, pt[bi * nb + e, g * pg + i], 0, 0))

    in_specs = [
        pl.BlockSpec((nb, t, hp), lambda bi, g, pt: (bi, 0, 0)),
        pl.BlockSpec((nb, t, MLA_KV_LORA), lambda bi, g, pt: (bi, 0, 0)),
        pl.BlockSpec((nb, t, MLA_ROPE), lambda bi, g, pt: (bi, 0, 0)),
        pl.BlockSpec(wabs.shape, lambda bi, g, pt: (0, 0)),
        pl.BlockSpec(wuv.shape, lambda bi, g, pt: (0, 0)),
    ]
    in_specs += [page_spec(PAGE_SIZE, MLA_KV_LORA, e, i) for e in range(nb) for i in range(pg)]
    in_specs += [page_spec(MLA_ROPE, PAGE_SIZE, e, i) for e in range(nb) for i in range(pg)]
    kern = functools.partial(_paged_kernel, ng=ng, pg=pg, t=t, nb=nb)
    return pl.pallas_call(
        kern,
        out_shape=jax.ShapeDtypeStruct((b, t, MLA_W), F32),
        grid_spec=pltpu.PrefetchScalarGridSpec(
            num_scalar_prefetch=1,
            grid=(b // nb, ng),
            in_specs=in_specs,
            out_specs=pl.BlockSpec((nb, t, MLA_W), lambda bi, g, pt: (bi, 0, 0)),
            scratch_shapes=[pltpu.VMEM((nb, rows, MLA_KV_LORA), BF16),
                            pltpu.VMEM((nb, rows, LANES), BF16),
                            pltpu.VMEM((nb, rows, 1), F32),
                            pltpu.VMEM((nb, rows, 1), F32),
                            pltpu.VMEM((nb, rows, MLA_KV_LORA), F32),
                            pltpu.VMEM((nb, pg * PAGE_SIZE, MLA_KV_LORA), BF16),
                            pltpu.VMEM((nb, MLA_ROPE, pg * PAGE_SIZE), BF16)],
        ),
        compiler_params=_cparams(("parallel", "arbitrary")),
        name="mla_paged",
    )(page_table, q, lat_new, kpe_new, wabs, wuv, *([lat_pool] * (nb * pg)), *([pe_pool] * (nb * pg)))


def _merge_kernel(h_ref, gate_ref, b0_ref, b1_ref, b2_ref, b3_ref, wbo_ref, wout_ref, o_ref):
    branches = (b0_ref, b1_ref, b2_ref, b3_ref)
    merged = None
    off = 0
    for i, br in enumerate(branches):
        wd = br.shape[1]
        proj = _dot(br[...].astype(BF16), wbo_ref[off:off + wd, :])
        term = jax.nn.sigmoid(gate_ref[:, i * D_MODEL:(i + 1) * D_MODEL].astype(F32)) * proj
        merged = term if merged is None else merged + term
        off += wd
    o_ref[...] = h_ref[...] + _dot(merged.astype(BF16), wout_ref[...])


def _merge(h, proj, branches, wbo, wout, *, tm):
    n = h.shape[0]
    row = lambda c: pl.BlockSpec((tm, c), lambda i: (i, 0))
    return pl.pallas_call(
        _merge_kernel,
        out_shape=jax.ShapeDtypeStruct((n, D_MODEL), F32),
        grid=(n // tm,),
        in_specs=[row(D_MODEL), row(GATE_W)] + [row(br.shape[1]) for br in branches]
        + [pl.BlockSpec(wbo.shape, lambda i: (0, 0)), pl.BlockSpec(wout.shape, lambda i: (0, 0))],
        out_specs=row(D_MODEL),
        compiler_params=_cparams(("parallel",)),
        name="branch_merge",
    )(h, proj, *branches, wbo, wout)


def _ffn_kernel(h_ref, g_ref, w1_ref, w2_ref, gf_ref, o_ref, *, tf, final_norm):
    h = h_ref[...]
    xn = _rms(h, g_ref[...]).astype(BF16)
    acc = None
    for j in range(w1_ref.shape[1] // tf):
        u = _dot(xn, w1_ref[:, j * tf:(j + 1) * tf])
        u = jnp.square(jnp.maximum(u, 0.0))
        part = _dot(u.astype(BF16), w2_ref[j * tf:(j + 1) * tf, :])
        acc = part if acc is None else acc + part
    out = h + acc
    if final_norm:
        out = _rms(out, gf_ref[...])
    o_ref[...] = out


def _ffn(h, g, w1, w2, gf, *, tm, tf, final_norm):
    n, d = h.shape
    f = w1.shape[1]
    kern = functools.partial(_ffn_kernel, tf=tf, final_norm=final_norm)
    return pl.pallas_call(
        kern,
        out_shape=jax.ShapeDtypeStruct((n, d), F32),
        grid=(n // tm,),
        in_specs=[
            pl.BlockSpec((tm, d), lambda i: (i, 0)),
            pl.BlockSpec((1, d), lambda i: (0, 0)),
            pl.BlockSpec((d, f), lambda i: (0, 0)),
            pl.BlockSpec((f, d), lambda i: (0, 0)),
            pl.BlockSpec((1, d), lambda i: (0, 0)),
        ],
        out_specs=pl.BlockSpec((tm, d), lambda i: (i, 0)),
        compiler_params=_cparams(("parallel",)),
        name="relu2_mlp",
    )(h, g, w1, w2, gf)


def _rot_partner(w, half):
    return jnp.concatenate([-w[..., half:], w[..., :half]], axis=-1)


def _pack_layer(w_in_l, conv_w_l, q_norm_l, w_uq_l, kv_norm_l, w_uk_l, w_uv_l, hgrn_norm_l,
                w_bo_l, w_out_l, g_mix_l, g_ffn_l, w_ff1_l, w_ff2_l):
    d = w_in_l.shape[0]
    offs = np.cumsum((0,) + IN_SIZES)
    col = lambda i: w_in_l[:, int(offs[i]):int(offs[i + 1])]
    rq, rk, rv, rg, cb, cc, cx, mcq, mckv, mkr, hq, hf, hi, hg, mg = [col(i) for i in range(15)]

    def halves(w):
        w4 = w.reshape(d, RET_HEADS, 2, RET_DK // 2)
        return w4[:, :, 0, :].reshape(d, -1), w4[:, :, 1, :].reshape(d, -1)

    rq1, rq2 = halves(rq)
    rk1, rk2 = halves(rk)
    mkr_rot = _rot_partner(mkr, MLA_ROPE // 2)
    pad = jnp.zeros((d, MLA_BLOCK_W - MLA_Q_LORA - MLA_KV_LORA - 2 * MLA_ROPE), w_in_l.dtype)
    packed = jnp.concatenate([mg, rv,
                              hq, hf, hi, hg, rq1, rq2, rk1, rk2, rg, cb, cc, cx,
                              mcq, mckv, mkr, mkr_rot, pad], axis=1).astype(BF16)

    qd = MLA_NOPE + MLA_ROPE
    wq = w_uq_l.reshape(MLA_Q_LORA, MLA_HEADS, qd)
    wq_nope = jnp.concatenate(
        [wq[:, :, :MLA_NOPE], jnp.zeros((MLA_Q_LORA, MLA_HEADS, MLA_HEAD_PAD - MLA_NOPE), wq.dtype)],
        axis=2).reshape(MLA_Q_LORA, MLA_HEADS * MLA_HEAD_PAD)
    wq_pe = wq[:, :, MLA_NOPE:]
    wq_rot = _rot_partner(wq_pe, MLA_ROPE // 2)
    e_q = np.zeros((MLA_HEADS * MLA_ROPE, MLA_HEADS * MLA_HEAD_PAD), np.float32)
    e_k = np.zeros((MLA_ROPE, MLA_HEADS * MLA_HEAD_PAD), np.float32)
    for h in range(MLA_HEADS):
        for i in range(MLA_ROPE):
            e_q[h * MLA_ROPE + i, h * MLA_HEAD_PAD + MLA_NOPE + i] = 1.0
            e_k[i, h * MLA_HEAD_PAD + MLA_NOPE + i] = 1.0
    wuk_pad = jnp.concatenate(
        [w_uk_l, jnp.zeros((MLA_KV_LORA, MLA_HEADS, MLA_HEAD_PAD - MLA_NOPE), w_uk_l.dtype)],
        axis=2).reshape(MLA_KV_LORA, MLA_HEADS * MLA_HEAD_PAD)
    wabs = jnp.zeros((MLA_HEADS, MLA_HEAD_PAD, MLA_KV_LORA + LANES), F32)
    wabs = wabs.at[:, :MLA_NOPE, :MLA_KV_LORA].set(jnp.transpose(w_uk_l, (1, 2, 0)))
    wabs = wabs.at[:, MLA_NOPE:MLA_NOPE + MLA_ROPE, MLA_KV_LORA:MLA_KV_LORA + MLA_ROPE].set(
        jnp.broadcast_to(jnp.eye(MLA_ROPE, dtype=F32), (MLA_HEADS, MLA_ROPE, MLA_ROPE)))
    wabs = wabs.reshape(MLA_HEADS * MLA_HEAD_PAD, MLA_KV_LORA + LANES)
    return {
        "w_in": packed,
        "g_mix": g_mix_l.reshape(1, d),
        "conv_w": conv_w_l,
        "q_norm": q_norm_l.reshape(1, -1),
        "kv_norm": kv_norm_l.reshape(1, -1),
        "wq_nope": wq_nope.astype(BF16),
        "wq_pe": wq_pe.reshape(MLA_Q_LORA, -1).astype(BF16),
        "wq_rot": wq_rot.reshape(MLA_Q_LORA, -1).astype(BF16),
        "e_q": jnp.asarray(e_q, BF16),
        "e_k": jnp.asarray(e_k, BF16),
        "wuk_pad": wuk_pad.astype(BF16),
        "wuv": w_uv_l.reshape(MLA_KV_LORA, MLA_W).astype(BF16),
        "wuv_t": w_uv_l.reshape(MLA_KV_LORA, MLA_W).T.astype(BF16),
        "wabs": wabs.astype(BF16),
        "hgrn_norm": hgrn_norm_l.reshape(1, -1),
        "w_bo": w_bo_l.astype(BF16),
        "w_out": w_out_l.astype(BF16),
        "g_ffn": g_ffn_l.reshape(1, d),
        "w_ff1": w_ff1_l.astype(BF16),
        "w_ff2": w_ff2_l.astype(BF16),
    }


def _rope_tables(pos):
    def cs(half):
        inv = ROPE_THETA ** (-jnp.arange(half, dtype=F32) / half)
        ang = pos.astype(F32)[:, None] * inv[None, :]
        return jnp.cos(ang), jnp.sin(ang)

    c_ret, s_ret = cs(RET_DK // 2)
    c_pe, s_pe = cs(MLA_ROPE // 2)
    two = lambda a: jnp.concatenate([a, a], axis=1)
    return {
        "cos_ret": jnp.tile(c_ret, (1, RET_HEADS)), "sin_ret": jnp.tile(s_ret, (1, RET_HEADS)),
        "cos_q": jnp.tile(two(c_pe), (1, MLA_HEADS)), "sin_q": jnp.tile(two(s_pe), (1, MLA_HEADS)),
        "cos_k": two(c_pe), "sin_k": two(s_pe),
    }


def _tiles(t):
    return {"conv": min(512, t), "hgrn": min(256, t), "prep": min(512, t), "tq": min(512, t), "tk": min(512, t)}


def _trunk_layer(h, tabs, ret_state, conv_buf, hgrn_state, lb, lw, *, b, t, attend, final_gain):
    n = b * t
    tl = _tiles(t)
    odt = BF16 if t % 16 == 0 else F32
    proj_a, proj_b = _inproj(h, lw["g_mix"], lw["w_in"], tm=min(n, 2048), adt=odt)
    o_ret, ret_new = _retention(proj_a, proj_b, tabs["cos_ret"], tabs["sin_ret"], ret_state, b=b, t=t, odt=odt)
    o_conv, conv_new = _short_conv(proj_b, lw["conv_w"], conv_buf, b=b, t=t, rows=tl["conv"], odt=odt)
    o_h, hgrn_new = _hgrn(proj_b, lb, lw["hgrn_norm"], hgrn_state, b=b, t=t, rows=tl["hgrn"], odt=odt)
    o_mla, ckv, kpe = attend(proj_b, lw, tabs, tl, odt)
    h = _merge(h, proj_a, (o_ret, o_conv, o_mla, o_h), lw["w_bo"], lw["w_out"], tm=min(n, 512))
    gf = final_gain if final_gain is not None else lw["g_ffn"]
    h = _ffn(h, lw["g_ffn"], lw["w_ff1"], lw["w_ff2"], gf, tm=min(n, 512), tf=512,
             final_norm=final_gain is not None)
    return h, ret_new, conv_new, hgrn_new, ckv, kpe


def kernel(x_prompt, x_sample, cache_mla_latent, cache_mla_rope, page_table,
           state_retention, state_conv, state_hgrn,
           norm_mix, w_in, conv_w, mla_q_norm, mla_w_uq, mla_kv_norm, mla_w_uk, mla_w_uv,
           hgrn_lb_logits, hgrn_out_norm, w_branch_out, w_out, norm_ffn, w_ff1, w_ff2, norm_final):
    bp, tp, d = x_prompt.shape
    bs, ts, _ = x_sample.shape
    depth = w_in.shape[0]
    past_len = page_table.shape[1] * PAGE_SIZE
    tabs_p = _rope_tables(jnp.arange(tp, dtype=jnp.int32))
    tabs_s = _rope_tables(past_len + jnp.arange(ts, dtype=jnp.int32))
    lb_p = jax.nn.softmax(hgrn_lb_logits.astype(F32), axis=0)
    lower_bounds = jnp.cumsum(lb_p, axis=0) - lb_p[0:1]
    gfin = norm_final.reshape(1, d)
    rope_pool_t = jnp.swapaxes(cache_mla_rope, 2, 3)

    hp = x_prompt.reshape(bp * tp, d)
    hs = x_sample.reshape(bs * ts, d)
    zeros_ret = jnp.zeros((bp, RET_HEADS, RET_DK, RET_DV), F32)
    zeros_conv = jnp.zeros((bp, CONV_K - 1, CONV_WIDTH), F32)
    zeros_hgrn = jnp.zeros((bp, HGRN_HEADS, HGRN_DK, HGRN_DV), F32)
    outs = {k: [] for k in ("lat_p", "rope_p", "lat_s", "rope_s", "ret_p", "ret_s",
                            "conv_p", "conv_s", "hg_p", "hg_s")}
    for l in range(depth):
        lw = _pack_layer(w_in[l], conv_w[l], mla_q_norm[l], mla_w_uq[l], mla_kv_norm[l], mla_w_uk[l],
                         mla_w_uv[l], hgrn_out_norm[l], w_branch_out[l], w_out[l], norm_mix[l],
                         norm_ffn[l], w_ff1[l], w_ff2[l])
        lb = lower_bounds[l].reshape(1, HGRN_W)
        last = gfin if l == depth - 1 else None

        def attend_prompt(proj, lw, tabs, tl, odt):
            q, ckv, kpe, kc, v = _mla_prep(proj, lw, tabs, n=bp * tp, t=tp, tm=tl["prep"], with_kv=True)
            o = _flash(q, kc, v, b=bp, t=tp, tq=tl["tq"], tk=tl["tk"], odt=odt)
            return o, ckv, kpe

        def attend_sample(proj, lw, tabs, tl, odt, layer=l):
            q, ckv, kpe = _mla_prep(proj, lw, tabs, n=bs * ts, t=ts, tm=tl["prep"], with_kv=False)
            o = _paged_attention(page_table, q.reshape(bs, ts, -1), ckv.reshape(bs, ts, -1),
                                 kpe.reshape(bs, ts, -1), lw["wabs"], lw["wuv"],
                                 cache_mla_latent, rope_pool_t, layer=layer,
                                 pg=math.gcd(16, page_table.shape[1]), nb=math.gcd(2, bs))
            return o.reshape(bs * ts, MLA_W), ckv, kpe

        hp, r, c, g, ckv, kpe = _trunk_layer(hp, tabs_p, zeros_ret, zeros_conv, zeros_hgrn, lb, lw,
                                             b=bp, t=tp, attend=attend_prompt, final_gain=last)
        outs["ret_p"].append(r); outs["conv_p"].append(c); outs["hg_p"].append(g)
        outs["lat_p"].append(ckv.reshape(bp, tp, -1)); outs["rope_p"].append(kpe.reshape(bp, tp, -1))
        hs, r, c, g, ckv, kpe = _trunk_layer(hs, tabs_s, state_retention[l], state_conv[l], state_hgrn[l],
                                             lb, lw, b=bs, t=ts, attend=attend_sample, final_gain=last)
        outs["ret_s"].append(r); outs["conv_s"].append(c); outs["hg_s"].append(g)
        outs["lat_s"].append(ckv.reshape(bs, ts, -1)); outs["rope_s"].append(kpe.reshape(bs, ts, -1))

    st = lambda k: jnp.stack(outs[k])
    return (hp.reshape(bp, tp, d), hs.reshape(bs, ts, d),
            st("lat_p"), st("rope_p"), st("lat_s"), st("rope_s"),
            st("ret_p"), st("ret_s"), st("conv_p"), st("conv_s"), st("hg_p"), st("hg_s"))
```

```python
import functools
import math

import numpy as np
import jax
import jax.numpy as jnp
from jax import lax
from jax.experimental import pallas as pl
from jax.experimental.pallas import tpu as pltpu

F32 = jnp.float32
BF16 = jnp.bfloat16

D_MODEL = 1024
PAGE_SIZE = 128
RET_HEADS, RET_DK, RET_DV, RET_CHUNK = 4, 64, 128, 128
CONV_WIDTH, CONV_K = 512, 3
MLA_HEADS, MLA_Q_LORA, MLA_KV_LORA, MLA_NOPE, MLA_ROPE, MLA_V = 8, 384, 256, 64, 32, 64
MLA_SCALE = (MLA_NOPE + MLA_ROPE) ** -0.5
HGRN_HEADS, HGRN_DK, HGRN_DV = 4, 128, 128
D_FF = 4 * D_MODEL
ROPE_THETA = 10000.0
NORM_EPS = 1e-6
N_BRANCH = 4
NEG_BIG = -1e30
LOG2E = math.log2(math.e)

RET_W = RET_HEADS * RET_DV
MLA_W = MLA_HEADS * MLA_V
HGRN_W = HGRN_HEADS * HGRN_DV
IN_SIZES = (
    RET_HEADS * RET_DK, RET_HEADS * RET_DK, RET_W, RET_W,
    CONV_WIDTH, CONV_WIDTH, CONV_WIDTH,
    MLA_Q_LORA, MLA_KV_LORA, MLA_ROPE,
    HGRN_HEADS * HGRN_DK, HGRN_HEADS * HGRN_DK, HGRN_W, HGRN_W,
    N_BRANCH * D_MODEL,
)

LANES = 128
SUBLANES = 8
VMEM_LIMIT_BYTES = 56 * 1024 * 1024

IN_TILE = 768
GATE_W = N_BRANCH * D_MODEL
RV_OFF = GATE_W
PROJ_A_W = GATE_W + RET_W
HG_BLOCK_W = 4 * HGRN_W
RET_QKG_W = 1024
RET_QKG_OFF = HG_BLOCK_W
CONV_BLOCK_W = 3 * CONV_WIDTH
CONV_OFF = RET_QKG_OFF + RET_QKG_W
MLA_BLOCK_W = 768
MLA_OFF = CONV_OFF + CONV_BLOCK_W
PROJ_B_W = MLA_OFF + MLA_BLOCK_W
MLA_HEAD_PAD = 128
HGRN_SUB = 16
HGRN_CHUNK = 64
HGRN_SAFE_LOG_DECAY = 160.0
KEY_PAD = 128


def _cparams(sem):
    return pltpu.CompilerParams(dimension_semantics=sem, vmem_limit_bytes=VMEM_LIMIT_BYTES)


def _rms(x, g):
    return x * lax.rsqrt(jnp.mean(x * x, axis=-1, keepdims=True) + NORM_EPS) * g


def _silu(x):
    return x * jax.nn.sigmoid(x)


def _dot(a, b):
    return jnp.dot(a, b, preferred_element_type=F32)


def _dot_nt(a, b):
    return lax.dot_general(a, b, (((1,), (1,)), ((), ())), preferred_element_type=F32)


def _dot_tn(a, b):
    return lax.dot_general(a, b, (((0,), (0,)), ((), ())), preferred_element_type=F32)


def _inproj_kernel(x_ref, g_ref, w_ref, oa_ref, ob_ref, xn_ref, *, na):
    j = pl.program_id(1)

    @pl.when(j == 0)
    def _():
        xn_ref[...] = _rms(x_ref[...], g_ref[...]).astype(BF16)

    @pl.when(j < na)
    def _():
        oa_ref[...] = _dot(xn_ref[...], w_ref[...]).astype(oa_ref.dtype)

    @pl.when(j >= na)
    def _():
        ob_ref[...] = _dot(xn_ref[...], w_ref[...])


def _inproj(x, g, w, *, tm, adt):
    n, d = x.shape
    tn = IN_TILE
    na = PROJ_A_W // tn
    nb = PROJ_B_W // tn
    return pl.pallas_call(
        functools.partial(_inproj_kernel, na=na),
        out_shape=(jax.ShapeDtypeStruct((n, PROJ_A_W), adt), jax.ShapeDtypeStruct((n, PROJ_B_W), F32)),
        grid=(n // tm, na + nb),
        in_specs=[
            pl.BlockSpec((tm, d), lambda i, j: (i, 0)),
            pl.BlockSpec((1, d), lambda i, j: (0, 0)),
            pl.BlockSpec((d, tn), lambda i, j: (0, j)),
        ],
        out_specs=(pl.BlockSpec((tm, tn), lambda i, j: (i, jnp.minimum(j, na - 1))),
                   pl.BlockSpec((tm, tn), lambda i, j: (i, jnp.maximum(j - na, 0)))),
        scratch_shapes=[pltpu.VMEM((tm, d), BF16)],
        compiler_params=_cparams(("parallel", "arbitrary")),
        name="in_proj",
    )(x, g, w)


def _ret_kernel(x_ref, v_ref, cos_ref, sin_ref, decay_ref, qdec_ref, kdec_ref, s0_ref,
                o_ref, sout_ref, s_ref, *, nt, cdec, c):
    t = pl.program_id(1)
    hw = RET_HEADS * RET_DK // 2
    per = RET_DK // 2

    @pl.when(t == 0)
    def _():
        s_ref[...] = jnp.zeros(s_ref.shape, F32)
        for h in range(RET_HEADS):
            s_ref[h, h * per:(h + 1) * per, :] = s0_ref[0, h, 0:per, :]
            s_ref[h, hw + h * per:hw + (h + 1) * per, :] = s0_ref[0, h, per:2 * per, :]

    lane = lax.broadcasted_iota(jnp.int32, (1, 2 * hw), 1)
    head_of_lane = (lane % hw) // per
    for sub in range(x_ref.shape[0] // c):
        r0 = sub * c
        x = x_ref[r0:r0 + c, :]
        cos = cos_ref[r0:r0 + c, :]
        sin = sin_ref[r0:r0 + c, :]
        q1, q2 = x[:, 0:hw], x[:, hw:2 * hw]
        k1, k2 = x[:, 2 * hw:3 * hw], x[:, 3 * hw:4 * hw]
        qr = jnp.concatenate([q1 * cos - q2 * sin, q1 * sin + q2 * cos], axis=1)
        kr = jnp.concatenate([k1 * cos - k2 * sin, k1 * sin + k2 * cos], axis=1) * (RET_DK ** -0.5)
        v = v_ref[r0:r0 + c, :]
        g = x[:, 4 * hw:4 * hw + RET_W]
        if c < KEY_PAD:
            kr = jnp.concatenate([kr, jnp.zeros((KEY_PAD - c, kr.shape[1]), F32)], axis=0)
            v = jnp.concatenate([v, jnp.zeros((KEY_PAD - c, v.shape[1]), v.dtype)], axis=0)
        kt = kr.T
        ktb = kt.astype(BF16)
        qms = [jnp.where(head_of_lane == h, qr, 0.0).astype(BF16) for h in range(RET_HEADS)]
        atts = [(_dot(qms[h], ktb) * decay_ref[h]).astype(BF16) for h in range(RET_HEADS)]
        for h in range(RET_HEADS):
            qm = qms[h]
            vh = v[:, h * RET_DV:(h + 1) * RET_DV].astype(BF16)
            sh = s_ref[h]
            o = _dot(atts[h], vh) + _dot(qm, sh.astype(BF16)) * qdec_ref[h]
            s_ref[h] = sh * cdec[h] + _dot((kt * kdec_ref[h]).astype(BF16), vh)
            o = o * lax.rsqrt(jnp.mean(o * o, axis=-1, keepdims=True) + NORM_EPS)
            gh = g[:, h * RET_DV:(h + 1) * RET_DV]
            o_ref[r0:r0 + c, h * RET_DV:(h + 1) * RET_DV] = (o * _silu(gh)).astype(o_ref.dtype)

    @pl.when(t == nt - 1)
    def _():
        for h in range(RET_HEADS):
            sout_ref[0, h, 0:per, :] = s_ref[h, h * per:(h + 1) * per, :]
            sout_ref[0, h, per:2 * per, :] = s_ref[h, hw + h * per:hw + (h + 1) * per, :]


def _retention(proj_a, proj_b, cos, sin, state, *, b, t, odt):
    c = math.gcd(t, RET_CHUNK)
    rows = c * math.gcd(t // c, 2)
    nt = t // rows
    ck = max(c, KEY_PAD)
    hw = RET_HEADS * RET_DK // 2
    hs = np.arange(RET_HEADS, dtype=np.float64)
    log_gamma = np.log1p(-np.exp2(-5.0 - hs))
    j = np.arange(c, dtype=np.float64)
    rel = j[:, None] - j[None, :]
    decay = np.where(rel >= 0, np.exp(log_gamma[:, None, None] * np.where(rel >= 0, rel, 0.0)), 0.0)
    decay_p = np.zeros((RET_HEADS, c, ck))
    decay_p[:, :, :c] = decay
    qdec = np.repeat(np.exp(log_gamma[:, None] * (j + 1.0))[:, :, None], RET_DV, axis=2)
    kdec = np.zeros((RET_HEADS, 1, ck))
    kdec[:, 0, :c] = np.exp(log_gamma[:, None] * (c - 1.0 - j))
    cdec = tuple(float(np.float32(np.exp(lg * c))) for lg in log_gamma)
    kern = functools.partial(_ret_kernel, nt=nt, cdec=cdec, c=c)
    return pl.pallas_call(
        kern,
        out_shape=(jax.ShapeDtypeStruct((b * t, RET_W), odt),
                   jax.ShapeDtypeStruct((b, RET_HEADS, RET_DK, RET_DV), F32)),
        grid=(b, nt),
        in_specs=[
            pl.BlockSpec((rows, RET_QKG_W), lambda i, k: (i * nt + k, RET_QKG_OFF // RET_QKG_W)),
            pl.BlockSpec((rows, RET_W), lambda i, k: (i * nt + k, RV_OFF // RET_W)),
            pl.BlockSpec((rows, hw), lambda i, k: (k, 0)),
            pl.BlockSpec((rows, hw), lambda i, k: (k, 0)),
            pl.BlockSpec((RET_HEADS, c, ck), lambda i, k: (0, 0, 0)),
            pl.BlockSpec((RET_HEADS, c, RET_DV), lambda i, k: (0, 0, 0)),
            pl.BlockSpec((RET_HEADS, 1, ck), lambda i, k: (0, 0, 0)),
            pl.BlockSpec((1, RET_HEADS, RET_DK, RET_DV), lambda i, k: (i, 0, 0, 0)),
        ],
        out_specs=(pl.BlockSpec((rows, RET_W), lambda i, k: (i * nt + k, 0)),
                   pl.BlockSpec((1, RET_HEADS, RET_DK, RET_DV), lambda i, k: (i, 0, 0, 0))),
        scratch_shapes=[pltpu.VMEM((RET_HEADS, RET_HEADS * RET_DK, RET_DV), F32)],
        compiler_params=_cparams(("parallel", "arbitrary")),
        name="retention",
    )(proj_b, proj_a, cos, sin, jnp.asarray(decay_p, F32), jnp.asarray(qdec, F32), jnp.asarray(kdec, F32), state)


def _conv_kernel(x_ref, w_ref, buf_ref, o_ref, bout_ref, carry_ref, *, nt):
    t = pl.program_id(1)
    w = CONV_WIDTH

    @pl.when(t == 0)
    def _():
        carry_ref[...] = jnp.zeros(carry_ref.shape, F32)
        carry_ref[SUBLANES - 2:SUBLANES, :] = buf_ref[0]

    x = x_ref[...]
    rows = x.shape[0]
    gate_b, gate_c, xin = x[:, 0:w], x[:, w:2 * w], x[:, 2 * w:3 * w]
    u = gate_c * xin
    row = lax.broadcasted_iota(jnp.int32, (rows, 1), 0)
    p1 = carry_ref[SUBLANES - 1:SUBLANES, :]
    p2 = carry_ref[SUBLANES - 2:SUBLANES - 1, :]
    u1 = jnp.where(row == 0, p1, pltpu.roll(u, 1, 0))
    u2 = jnp.where(row == 0, p2, jnp.where(row == 1, p1, pltpu.roll(u, 2, 0)))
    y = u2 * w_ref[0:1, :] + u1 * w_ref[1:2, :] + u * w_ref[2:3, :]
    o_ref[...] = (gate_b * y).astype(o_ref.dtype)
    carry_ref[...] = u[rows - SUBLANES:rows, :]

    @pl.when(t == nt - 1)
    def _():
        bout_ref[0] = u[rows - 2:rows, :]


def _short_conv(proj, w, buf, *, b, t, rows, odt):
    nt = t // rows
    kern = functools.partial(_conv_kernel, nt=nt)
    return pl.pallas_call(
        kern,
        out_shape=(jax.ShapeDtypeStruct((b * t, CONV_WIDTH), odt),
                   jax.ShapeDtypeStruct((b, CONV_K - 1, CONV_WIDTH), F32)),
        grid=(b, nt),
        in_specs=[
            pl.BlockSpec((rows, CONV_BLOCK_W), lambda i, k: (i * nt + k, CONV_OFF // CONV_BLOCK_W)),
            pl.BlockSpec((CONV_K, CONV_WIDTH), lambda i, k: (0, 0)),
            pl.BlockSpec((1, CONV_K - 1, CONV_WIDTH), lambda i, k: (i, 0, 0)),
        ],
        out_specs=(pl.BlockSpec((rows, CONV_WIDTH), lambda i, k: (i * nt + k, 0)),
                   pl.BlockSpec((1, CONV_K - 1, CONV_WIDTH), lambda i, k: (i, 0, 0))),
        scratch_shapes=[pltpu.VMEM((SUBLANES, CONV_WIDTH), F32)],
        compiler_params=_cparams(("parallel", "arbitrary")),
        name="short_conv",
    )(proj, w, buf)


def _split3(x):
    hi = x.astype(BF16)
    r = x - hi.astype(F32)
    mid = r.astype(BF16)
    lo = (r - mid.astype(F32)).astype(BF16)
    return hi, mid, lo


def _hgrn_intra_exact(gcum, qh, kk, hv, g_ref, k_ref, oi_ref):
    ell = gcum.shape[0]
    dk = HGRN_DK
    g_ref[...] = gcum
    k_ref[...] = kk
    nsub = ell // HGRN_SUB
    lane = lax.broadcasted_iota(jnp.int32, (1, KEY_PAD), 1)
    key_row = lax.broadcasted_iota(jnp.int32, (ell, 1), 0)
    sub_row = lax.broadcasted_iota(jnp.int32, (HGRN_SUB, 1), 0)
    a_blocks = [[None] * nsub for _ in range(HGRN_HEADS)]
    for j in range(nsub):
        r0 = j * HGRN_SUB
        gq = gcum[r0:r0 + HGRN_SUB, :]
        qq = qh[r0:r0 + HGRN_SUB, :]

        def diag_body(s, carry, r0=r0, gq=gq, qq=qq):
            gs = g_ref[pl.ds(r0 + s, 1), :]
            ks = k_ref[pl.ds(r0 + s, 1), :]
            valid = sub_row >= s
            p = jnp.where(valid, jnp.exp(jnp.minimum(gq - gs, 0.0)) * qq * ks, 0.0)
            out = []
            for h in range(HGRN_HEADS):
                a = jnp.sum(p[:, h * dk:(h + 1) * dk], axis=-1, keepdims=True)
                out.append(carry[h] + jnp.where(lane == r0 + s, a, 0.0))
            return tuple(out)

        init = tuple(jnp.zeros((HGRN_SUB, KEY_PAD), F32) for _ in range(HGRN_HEADS))
        diag = lax.fori_loop(0, HGRN_SUB, diag_body, init)
        if j > 0:
            gb = gcum[r0 - 1:r0, :]
            q_t = (qq * jnp.exp(gq - gb)).astype(BF16)
            k_t = jnp.where(key_row < r0, kk * jnp.exp(jnp.minimum(gb - gcum, 0.0)), 0.0)
            if ell < KEY_PAD:
                k_t = jnp.concatenate([k_t, jnp.zeros((KEY_PAD - ell, k_t.shape[1]), F32)], axis=0)
            k_t = k_t.astype(BF16)
        for h in range(HGRN_HEADS):
            blk = diag[h]
            if j > 0:
                blk = blk + _dot_nt(q_t[:, h * dk:(h + 1) * dk], k_t[:, h * dk:(h + 1) * dk])
            a_blocks[h][j] = blk
    for h in range(HGRN_HEADS):
        a_h = a_blocks[h][0] if nsub == 1 else jnp.concatenate(a_blocks[h], axis=0)
        vh = hv[:, h * HGRN_DV:(h + 1) * HGRN_DV]
        if ell < KEY_PAD:
            vh = jnp.concatenate([vh, jnp.zeros((KEY_PAD - ell, HGRN_DV), F32)], axis=0)
        oi_ref[h] = _dot(a_h.astype(BF16), vh.astype(BF16))


def _hgrn_intra_fast(parts, oi_ref):
    dk = HGRN_DK
    ell = parts[0][0].shape[0]
    causal = lax.broadcasted_iota(jnp.int32, (ell, ell), 0) >= lax.broadcasted_iota(jnp.int32, (ell, ell), 1)
    scores = []
    for gcum, g_end, qh, kk, _, _, _ in parts:
        gm = 0.5 * g_end
        q_t = (qh * jnp.exp(gcum - gm)).astype(BF16)
        k_t = (kk * jnp.exp(gm - gcum)).astype(BF16)
        scores.append([_dot_nt(q_t[:, h * dk:(h + 1) * dk], k_t[:, h * dk:(h + 1) * dk])
                       for h in range(HGRN_HEADS)])
    masked = [[jnp.where(causal, a, 0.0).astype(BF16) for a in row] for row in scores]
    for c, prt in enumerate(parts):
        hv = prt[4]
        for h in range(HGRN_HEADS):
            oi_ref[c, h] = _dot(masked[c][h], hv[:, h * HGRN_DV:(h + 1) * HGRN_DV].astype(BF16))


def _hgrn_kernel(x_ref, lb_ref, gn_ref, s0_ref, o_ref, sout_ref,
                 st_ref, g_ref, k_ref, oi_ref, *, nt, chunk):
    t = pl.program_id(1)
    w = HGRN_W
    dk = HGRN_DK

    @pl.when(t == 0)
    def _():
        for h in range(HGRN_HEADS):
            st_ref[h] = s0_ref[0, h].T

    rows = x_ref.shape[0]
    lb = lb_ref[...]
    gn = gn_ref[...]
    ri = lax.broadcasted_iota(jnp.int32, (chunk, chunk), 0)
    ci = lax.broadcasted_iota(jnp.int32, (chunk, chunk), 1)
    tri = jnp.where(ri >= ci, 1.0, 0.0).astype(BF16)
    parts = []
    for c in range(max(rows // chunk, 1)):
        if rows >= chunk:
            x = x_ref[c * chunk:(c + 1) * chunk, :]
            live = chunk
        else:
            x = jnp.concatenate([x_ref[...], jnp.zeros((chunk - rows, x_ref.shape[1]), F32)], axis=0)
            live = rows
        hq, hf, hv, hg = x[:, 0:w], x[:, w:2 * w], x[:, 2 * w:3 * w], x[:, 3 * w:4 * w]
        f_gate = lb + (1.0 - lb) * jax.nn.sigmoid(hf)
        log_f = jnp.log(f_gate)
        kk = 1.0 - f_gate
        if live < chunk:
            ok = lax.broadcasted_iota(jnp.int32, (chunk, 1), 0) < live
            log_f = jnp.where(ok, log_f, 0.0)
            kk = jnp.where(ok, kk, 0.0)
        qh = _silu(hq)
        p_hi, p_mid, p_lo = _split3(log_f)
        gcum = _dot(tri, p_hi) + _dot(tri, p_mid) + _dot(tri, p_lo)
        parts.append((gcum, gcum[chunk - 1:chunk, :], qh, kk, hv, hg, live))

    g_min = parts[0][1]
    for prt in parts[1:]:
        g_min = jnp.minimum(g_min, prt[1])
    safe = jnp.min(g_min) > -HGRN_SAFE_LOG_DECAY

    @pl.when(safe)
    def _():
        _hgrn_intra_fast(parts, oi_ref)

    @pl.when(jnp.logical_not(safe))
    def _():
        for c, (gcum, g_end, qh, kk, hv, _, _) in enumerate(parts):
            _hgrn_intra_exact(gcum, qh, kk, hv, g_ref, k_ref, oi_ref.at[c])

    for c, (gcum, g_end, qh, kk, hv, hg, live) in enumerate(parts):
        q_in = (qh * jnp.exp(gcum)).astype(BF16)
        k_out = (kk * jnp.exp(g_end - gcum)).astype(BF16)
        e_end = jnp.exp(g_end)
        for h in range(HGRN_HEADS):
            sl = slice(h * dk, (h + 1) * dk)
            st = st_ref[h]
            o = oi_ref[c, h] + _dot_nt(q_in[:, sl], st.astype(BF16))
            st_ref[h] = st * e_end[:, sl] + _dot_tn(hv[:, h * HGRN_DV:(h + 1) * HGRN_DV].astype(BF16), k_out[:, sl])
            o = _rms(o, gn) * _silu(hg[:, h * HGRN_DV:(h + 1) * HGRN_DV])
            o_ref[c * chunk:c * chunk + live, h * HGRN_DV:(h + 1) * HGRN_DV] = o[0:live, :].astype(o_ref.dtype)

    @pl.when(t == nt - 1)
    def _():
        for h in range(HGRN_HEADS):
            sout_ref[0, h] = st_ref[h].T


def _hgrn(proj, lb, gnorm, state, *, b, t, rows, odt):
    nt = t // rows
    chunk = max(min(rows, HGRN_CHUNK), HGRN_SUB)
    nchunk = max(rows // chunk, 1)
    kern = functools.partial(_hgrn_kernel, nt=nt, chunk=chunk)
    return pl.pallas_call(
        kern,
        out_shape=(jax.ShapeDtypeStruct((b * t, HGRN_W), odt),
                   jax.ShapeDtypeStruct((b, HGRN_HEADS, HGRN_DK, HGRN_DV), F32)),
        grid=(b, nt),
        in_specs=[
            pl.BlockSpec((rows, HG_BLOCK_W), lambda i, k: (i * nt + k, 0)),
            pl.BlockSpec((1, HGRN_W), lambda i, k: (0, 0)),
            pl.BlockSpec((1, HGRN_DV), lambda i, k: (0, 0)),
            pl.BlockSpec((1, HGRN_HEADS, HGRN_DK, HGRN_DV), lambda i, k: (i, 0, 0, 0)),
        ],
        out_specs=(pl.BlockSpec((rows, HGRN_W), lambda i, k: (i * nt + k, 0)),
                   pl.BlockSpec((1, HGRN_HEADS, HGRN_DK, HGRN_DV), lambda i, k: (i, 0, 0, 0))),
        scratch_shapes=[pltpu.VMEM((HGRN_HEADS, HGRN_DV, HGRN_DK), F32),
                        pltpu.VMEM((chunk, HGRN_W), F32),
                        pltpu.VMEM((chunk, HGRN_W), F32),
                        pltpu.VMEM((nchunk, HGRN_HEADS, chunk, HGRN_DV), F32)],
        compiler_params=_cparams(("parallel", "arbitrary")),
        name="hgrn2",
    )(proj, lb, gnorm, state)


def _mla_prep_kernel(x_ref, qn_ref, kvn_ref, wqn_ref, wqp_ref, wqr_ref, eq_ref,
                     cq_ref, sq_ref, ck_ref, sk_ref, *rest, with_kv):
    if with_kv:
        wuk_ref, ek_ref, wuv_ref, q_ref, lat_ref, kpe_ref, kc_ref, v_ref = rest
    else:
        q_ref, lat_ref, kpe_ref = rest
    x = x_ref[...]
    mcq = x[:, 0:MLA_Q_LORA]
    mckv = x[:, MLA_Q_LORA:MLA_Q_LORA + MLA_KV_LORA]
    o = MLA_Q_LORA + MLA_KV_LORA
    mkr = x[:, o:o + MLA_ROPE]
    mkr_rot = x[:, o + MLA_ROPE:o + 2 * MLA_ROPE]
    xq = _rms(mcq, qn_ref[...]).astype(BF16)
    q_pe = _dot(xq, wqp_ref[...]) * cq_ref[...] + _dot(xq, wqr_ref[...]) * sq_ref[...]
    q_cat = _dot(xq, wqn_ref[...]) + _dot(q_pe.astype(BF16), eq_ref[...])
    q_ref[...] = (q_cat * (MLA_SCALE * LOG2E)).astype(q_ref.dtype)
    ckv = _rms(mckv, kvn_ref[...])
    lat_ref[...] = ckv
    kpe = mkr * ck_ref[...] + mkr_rot * sk_ref[...]
    kpe_ref[...] = kpe
    if with_kv:
        cb = ckv.astype(BF16)
        kc_ref[...] = (_dot(cb, wuk_ref[...]) + _dot(kpe.astype(BF16), ek_ref[...])).astype(BF16)
        v_ref[0] = _dot_nt(wuv_ref[...], cb).astype(BF16)


def _mla_prep(proj, lw, tabs, *, n, t, tm, with_kv):
    rows = min(tm, t)
    nt = t // rows
    hp = MLA_HEADS * MLA_HEAD_PAD
    qp_w = MLA_HEADS * MLA_ROPE
    full = lambda r, c: pl.BlockSpec((r, c), lambda i: (0, 0))
    tab = lambda c: pl.BlockSpec((rows, c), lambda i: (i % nt, 0))
    in_specs = [
        pl.BlockSpec((rows, MLA_BLOCK_W), lambda i: (i, MLA_OFF // MLA_BLOCK_W)),
        full(1, MLA_Q_LORA), full(1, MLA_KV_LORA),
        full(MLA_Q_LORA, hp), full(MLA_Q_LORA, qp_w), full(MLA_Q_LORA, qp_w), full(qp_w, hp),
        tab(qp_w), tab(qp_w), tab(MLA_ROPE), tab(MLA_ROPE),
    ]
    args = [proj, lw["q_norm"], lw["kv_norm"], lw["wq_nope"], lw["wq_pe"], lw["wq_rot"], lw["e_q"],
            tabs["cos_q"], tabs["sin_q"], tabs["cos_k"], tabs["sin_k"]]
    q_dtype = BF16 if with_kv else F32
    out_shape = [jax.ShapeDtypeStruct((n, hp), q_dtype),
                 jax.ShapeDtypeStruct((n, MLA_KV_LORA), F32),
                 jax.ShapeDtypeStruct((n, MLA_ROPE), F32)]
    out_specs = [pl.BlockSpec((rows, hp), lambda i: (i, 0)),
                 pl.BlockSpec((rows, MLA_KV_LORA), lambda i: (i, 0)),
                 pl.BlockSpec((rows, MLA_ROPE), lambda i: (i, 0))]
    if with_kv:
        in_specs += [full(MLA_KV_LORA, hp), full(MLA_ROPE, hp), full(MLA_W, MLA_KV_LORA)]
        args += [lw["wuk_pad"], lw["e_k"], lw["wuv_t"]]
        out_shape += [jax.ShapeDtypeStruct((n, hp), BF16), jax.ShapeDtypeStruct((n // t, MLA_W, t), BF16)]
        out_specs += [pl.BlockSpec((rows, hp), lambda i: (i, 0)),
                      pl.BlockSpec((1, MLA_W, rows), lambda i: (i // nt, 0, i % nt))]
    return pl.pallas_call(
        functools.partial(_mla_prep_kernel, with_kv=with_kv),
        out_shape=tuple(out_shape),
        grid=(n // rows,),
        in_specs=in_specs,
        out_specs=tuple(out_specs),
        compiler_params=_cparams(("parallel",)),
        name="mla_prep",
    )(*args)


def _flash_kernel(q_ref, k_ref, vt_ref, o_ref, m_ref, l_ref, acc_ref, *, tq, tk):
    qi = pl.program_id(1)
    kj = pl.program_id(2)
    hp = MLA_HEAD_PAD

    @pl.when(kj == 0)
    def _():
        m_ref[...] = jnp.full(m_ref.shape, NEG_BIG, F32)
        l_ref[...] = jnp.zeros(l_ref.shape, F32)
        acc_ref[...] = jnp.zeros(acc_ref.shape, F32)

    def step(mask):
        for h in range(MLA_HEADS):
            q = q_ref[:, h * hp:(h + 1) * hp]
            k = k_ref[:, h * hp:(h + 1) * hp]
            s = _dot_nt(k, q)
            if mask is not None:
                s = jnp.where(mask, s, NEG_BIG)
            m_old = m_ref[h]
            m_new = jnp.maximum(m_old, jnp.max(s, axis=0, keepdims=True))
            alpha = jnp.exp2(m_old - m_new)
            p = jnp.exp2(s - m_new)
            l_ref[h] = l_ref[h] * alpha + jnp.sum(p, axis=0, keepdims=True)
            vt = vt_ref[0, h * MLA_V:(h + 1) * MLA_V, :]
            acc_ref[h] = acc_ref[h] * alpha + _dot(vt, p.astype(BF16))
            m_ref[h] = m_new

    @pl.when(kj < qi)
    def _():
        step(None)

    @pl.when(kj == qi)
    def _():
        step(lax.broadcasted_iota(jnp.int32, (tk, tq), 0) <= lax.broadcasted_iota(jnp.int32, (tk, tq), 1))
        for h in range(MLA_HEADS):
            acc_ref[h] = acc_ref[h] / l_ref[h]
        o_ref[...] = acc_ref[...].reshape(MLA_W, tq).T.astype(o_ref.dtype)


def _flash(q, k, vt, *, b, t, tq, tk, odt):
    nq, nk = t // tq, t // tk
    hp = MLA_HEADS * MLA_HEAD_PAD
    kern = functools.partial(_flash_kernel, tq=tq, tk=tk)
    return pl.pallas_call(
        kern,
        out_shape=jax.ShapeDtypeStruct((b * t, MLA_W), odt),
        grid=(b, nq, nk),
        in_specs=[
            pl.BlockSpec((tq, hp), lambda i, a, c: (i * nq + a, 0)),
            pl.BlockSpec((tk, hp), lambda i, a, c: (i * nk + jnp.minimum(c, a), 0)),
            pl.BlockSpec((1, MLA_W, tk), lambda i, a, c: (i, 0, jnp.minimum(c, a))),
        ],
        out_specs=pl.BlockSpec((tq, MLA_W), lambda i, a, c: (i * nq + a, 0)),
        scratch_shapes=[pltpu.VMEM((MLA_HEADS, 1, tq), F32),
                        pltpu.VMEM((MLA_HEADS, 1, tq), F32),
                        pltpu.VMEM((MLA_HEADS, MLA_V, tq), F32)],
        compiler_params=_cparams(("parallel", "parallel", "arbitrary")),
        name="mla_flash",
    )(q, k, vt)


def _paged_kernel(pt_ref, q_ref, lat_new_ref, kpe_new_ref, wabs_ref, wuv_ref, *rest,
                  ng, pg, t, nb):
    npg = nb * pg
    lat_refs = rest[0:npg]
    pe_refs = rest[npg:2 * npg]
    o_ref = rest[2 * npg]
    ql_ref, qp_ref, m_ref, l_ref, acc_ref, c_ref, rt_ref = rest[2 * npg + 1:]
    g = pl.program_id(1)
    hp = MLA_HEAD_PAD
    rows = MLA_HEADS * t

    @pl.when(g == 0)
    def _():
        lane_head = lax.broadcasted_iota(jnp.int32, (1, MLA_HEADS * hp), 1) // hp
        for e in range(nb):
            q = q_ref[e]
            q_exp = jnp.concatenate(
                [jnp.where(lane_head == h, q, 0.0) for h in range(MLA_HEADS)], axis=0).astype(BF16)
            q_abs = _dot(q_exp, wabs_ref[...])
            ql_ref[e] = q_abs[:, 0:MLA_KV_LORA].astype(BF16)
            qp_ref[e] = q_abs[:, MLA_KV_LORA:MLA_KV_LORA + LANES].astype(BF16)
        m_ref[...] = jnp.full(m_ref.shape, NEG_BIG, F32)
        l_ref[...] = jnp.zeros(l_ref.shape, F32)
        acc_ref[...] = jnp.zeros(acc_ref.shape, F32)

    def update(e, s, c):
        m_old = m_ref[e]
        m_new = jnp.maximum(m_old, jnp.max(s, axis=-1, keepdims=True))
        alpha = jnp.exp2(m_old - m_new)
        p = jnp.exp2(s - m_new)
        l_ref[e] = l_ref[e] * alpha + jnp.sum(p, axis=-1, keepdims=True)
        acc_ref[e] = acc_ref[e] * alpha + _dot(p.astype(BF16), c)
        m_ref[e] = m_new

    for e in range(nb):
        for i in range(pg):
            c_ref[e, i * PAGE_SIZE:(i + 1) * PAGE_SIZE, :] = lat_refs[e * pg + i][0, 0].astype(BF16)
            rt_ref[e, :, i * PAGE_SIZE:(i + 1) * PAGE_SIZE] = pe_refs[e * pg + i][0, 0].astype(BF16)
    s_all = [_dot_nt(ql_ref[e], c_ref[e]) + _dot(qp_ref[e, :, 0:MLA_ROPE], rt_ref[e]) for e in range(nb)]
    for e in range(nb):
        update(e, s_all[e], c_ref[e])

    @pl.when(g == ng - 1)
    def _():
        qtok = lax.broadcasted_iota(jnp.int32, (rows, KEY_PAD), 0) % t
        kidx = lax.broadcasted_iota(jnp.int32, (rows, KEY_PAD), 1)
        lane_head = lax.broadcasted_iota(jnp.int32, (1, MLA_W), 1) // MLA_V
        for e in range(nb):
            cn = jnp.concatenate([lat_new_ref[e], jnp.zeros((KEY_PAD - t, MLA_KV_LORA), F32)], axis=0).astype(BF16)
            rn = jnp.concatenate([kpe_new_ref[e], jnp.zeros((KEY_PAD - t, MLA_ROPE), F32)], axis=0).astype(BF16)
            s = _dot_nt(ql_ref[e], cn) + _dot_nt(qp_ref[e, :, 0:MLA_ROPE], rn)
            update(e, jnp.where(kidx <= qtok, s, NEG_BIG), cn)
            o_lat = acc_ref[e] / l_ref[e]
            res = _dot(o_lat.astype(BF16), wuv_ref[...])
            out = jnp.zeros((t, MLA_W), F32)
            for h in range(MLA_HEADS):
                out = out + jnp.where(lane_head == h, res[h * t:(h + 1) * t, :], 0.0)
            o_ref[e] = out


def _paged_attention(page_table, q, lat_new, kpe_new, wabs, wuv, lat_pool, pe_pool, *, layer, pg, nb):
    b, t, hp = q.shape
    n_pages = page_table.shape[1]
    ng = n_pages // pg
    rows = MLA_HEADS * t

    def page_spec(r, c, e, i):
        return pl.BlockSpec((1, 1, r, c), lambda bi, g, pt: (layer, pt[bi * nb + e, g * pg + i], 0, 0))

    in_specs = [
        pl.BlockSpec((nb, t, hp), lambda bi, g, pt: (bi, 0, 0)),
        pl.BlockSpec((nb, t, MLA_KV_LORA), lambda bi, g, pt: (bi, 0, 0)),
        pl.BlockSpec((nb, t, MLA_ROPE), lambda bi, g, pt: (bi, 0, 0)),
        pl.BlockSpec(wabs.shape, lambda bi, g, pt: (0, 0)),
        pl.BlockSpec(wuv.shape, lambda bi, g, pt: (0, 0)),
    ]
    in_specs += [page_spec(PAGE_SIZE, MLA_KV_LORA, e, i) for e in range(nb) for i in range(pg)]
    in_specs += [page_spec(MLA_ROPE, PAGE_SIZE, e, i) for e in range(nb) for i in range(pg)]
    kern = functools.partial(_paged_kernel, ng=ng, pg=pg, t=t, nb=nb)
    return pl.pallas_call(
        kern,
        out_shape=jax.ShapeDtypeStruct((b, t, MLA_W), F32),
        grid_spec=pltpu.PrefetchScalarGridSpec(
            num_scalar_prefetch=1,
            grid=(b // nb, ng),
            in_specs=in_specs,
            out_specs=pl.BlockSpec((nb, t, MLA_W), lambda bi, g, pt: (bi, 0, 0)),
            scratch_shapes=[pltpu.VMEM((nb, rows, MLA_KV_LORA), BF16),
                            pltpu.VMEM((nb, rows, LANES), BF16),
                            pltpu.VMEM((nb, rows, 1), F32),
                            pltpu.VMEM((nb, rows, 1), F32),
                            pltpu.VMEM((nb, rows, MLA_KV_LORA), F32),
                            pltpu.VMEM((nb, pg * PAGE_SIZE, MLA_KV_LORA), BF16),
                            pltpu.VMEM((nb, MLA_ROPE, pg * PAGE_SIZE), BF16)],
        ),
        compiler_params=_cparams(("parallel", "arbitrary")),
        name="mla_paged",
    )(page_table, q, lat_new, kpe_new, wabs, wuv, *([lat_pool] * (nb * pg)), *([pe_pool] * (nb * pg)))


def _merge_kernel(h_ref, gate_ref, b0_ref, b1_ref, b2_ref, b3_ref, wbo_ref, wout_ref, o_ref):
    branches = (b0_ref, b1_ref, b2_ref, b3_ref)
    merged = None
    off = 0
    for i, br in enumerate(branches):
        wd = br.shape[1]
        proj = _dot(br[...].astype(BF16), wbo_ref[off:off + wd, :])
        term = jax.nn.sigmoid(gate_ref[:, i * D_MODEL:(i + 1) * D_MODEL].astype(F32)) * proj
        merged = term if merged is None else merged + term
        off += wd
    o_ref[...] = h_ref[...] + _dot(merged.astype(BF16), wout_ref[...])


def _merge(h, proj, branches, wbo, wout, *, tm):
    n = h.shape[0]
    row = lambda c: pl.BlockSpec((tm, c), lambda i: (i, 0))
    return pl.pallas_call(
        _merge_kernel,
        out_shape=jax.ShapeDtypeStruct((n, D_MODEL), F32),
        grid=(n // tm,),
        in_specs=[row(D_MODEL), row(GATE_W)] + [row(br.shape[1]) for br in branches]
        + [pl.BlockSpec(wbo.shape, lambda i: (0, 0)), pl.BlockSpec(wout.shape, lambda i: (0, 0))],
        out_specs=row(D_MODEL),
        compiler_params=_cparams(("parallel",)),
        name="branch_merge",
    )(h, proj, *branches, wbo, wout)


def _ffn_kernel(h_ref, g_ref, w1_ref, w2_ref, gf_ref, o_ref, *, tf, final_norm):
    h = h_ref[...]
    xn = _rms(h, g_ref[...]).astype(BF16)
    acc = None
    for j in range(w1_ref.shape[1] // tf):
        u = _dot(xn, w1_ref[:, j * tf:(j + 1) * tf])
        u = jnp.square(jnp.maximum(u, 0.0))
        part = _dot(u.astype(BF16), w2_ref[j * tf:(j + 1) * tf, :])
        acc = part if acc is None else acc + part
    out = h + acc
    if final_norm:
        out = _rms(out, gf_ref[...])
    o_ref[...] = out


def _ffn(h, g, w1, w2, gf, *, tm, tf, final_norm):
    n, d = h.shape
    f = w1.shape[1]
    kern = functools.partial(_ffn_kernel, tf=tf, final_norm=final_norm)
    return pl.pallas_call(
        kern,
        out_shape=jax.ShapeDtypeStruct((n, d), F32),
        grid=(n // tm,),
        in_specs=[
            pl.BlockSpec((tm, d), lambda i: (i, 0)),
            pl.BlockSpec((1, d), lambda i: (0, 0)),
            pl.BlockSpec((d, f), lambda i: (0, 0)),
            pl.BlockSpec((f, d), lambda i: (0, 0)),
            pl.BlockSpec((1, d), lambda i: (0, 0)),
        ],
        out_specs=pl.BlockSpec((tm, d), lambda i: (i, 0)),
        compiler_params=_cparams(("parallel",)),
        name="relu2_mlp",
    )(h, g, w1, w2, gf)


def _rot_partner(w, half):
    return jnp.concatenate([-w[..., half:], w[..., :half]], axis=-1)


def _pack_layer(w_in_l, conv_w_l, q_norm_l, w_uq_l, kv_norm_l, w_uk_l, w_uv_l, hgrn_norm_l,
                w_bo_l, w_out_l, g_mix_l, g_ffn_l, w_ff1_l, w_ff2_l):
    d = w_in_l.shape[0]
    offs = np.cumsum((0,) + IN_SIZES)
    col = lambda i: w_in_l[:, int(offs[i]):int(offs[i + 1])]
    rq, rk, rv, rg, cb, cc, cx, mcq, mckv, mkr, hq, hf, hi, hg, mg = [col(i) for i in range(15)]

    def halves(w):
        w4 = w.reshape(d, RET_HEADS, 2, RET_DK // 2)
        return w4[:, :, 0, :].reshape(d, -1), w4[:, :, 1, :].reshape(d, -1)

    rq1, rq2 = halves(rq)
    rk1, rk2 = halves(rk)
    mkr_rot = _rot_partner(mkr, MLA_ROPE // 2)
    pad = jnp.zeros((d, MLA_BLOCK_W - MLA_Q_LORA - MLA_KV_LORA - 2 * MLA_ROPE), w_in_l.dtype)
    packed = jnp.concatenate([mg, rv,
                              hq, hf, hi, hg, rq1, rq2, rk1, rk2, rg, cb, cc, cx,
                              mcq, mckv, mkr, mkr_rot, pad], axis=1).astype(BF16)

    qd = MLA_NOPE + MLA_ROPE
    wq = w_uq_l.reshape(MLA_Q_LORA, MLA_HEADS, qd)
    wq_nope = jnp.concatenate(
        [wq[:, :, :MLA_NOPE], jnp.zeros((MLA_Q_LORA, MLA_HEADS, MLA_HEAD_PAD - MLA_NOPE), wq.dtype)],
        axis=2).reshape(MLA_Q_LORA, MLA_HEADS * MLA_HEAD_PAD)
    wq_pe = wq[:, :, MLA_NOPE:]
    wq_rot = _rot_partner(wq_pe, MLA_ROPE // 2)
    e_q = np.zeros((MLA_HEADS * MLA_ROPE, MLA_HEADS * MLA_HEAD_PAD), np.float32)
    e_k = np.zeros((MLA_ROPE, MLA_HEADS * MLA_HEAD_PAD), np.float32)
    for h in range(MLA_HEADS):
        for i in range(MLA_ROPE):
            e_q[h * MLA_ROPE + i, h * MLA_HEAD_PAD + MLA_NOPE + i] = 1.0
            e_k[i, h * MLA_HEAD_PAD + MLA_NOPE + i] = 1.0
    wuk_pad = jnp.concatenate(
        [w_uk_l, jnp.zeros((MLA_KV_LORA, MLA_HEADS, MLA_HEAD_PAD - MLA_NOPE), w_uk_l.dtype)],
        axis=2).reshape(MLA_KV_LORA, MLA_HEADS * MLA_HEAD_PAD)
    wabs = jnp.zeros((MLA_HEADS, MLA_HEAD_PAD, MLA_KV_LORA + LANES), F32)
    wabs = wabs.at[:, :MLA_NOPE, :MLA_KV_LORA].set(jnp.transpose(w_uk_l, (1, 2, 0)))
    wabs = wabs.at[:, MLA_NOPE:MLA_NOPE + MLA_ROPE, MLA_KV_LORA:MLA_KV_LORA + MLA_ROPE].set(
        jnp.broadcast_to(jnp.eye(MLA_ROPE, dtype=F32), (MLA_HEADS, MLA_ROPE, MLA_ROPE)))
    wabs = wabs.reshape(MLA_HEADS * MLA_HEAD_PAD, MLA_KV_LORA + LANES)
    return {
        "w_in": packed,
        "g_mix": g_mix_l.reshape(1, d),
        "conv_w": conv_w_l,
        "q_norm": q_norm_l.reshape(1, -1),
        "kv_norm": kv_norm_l.reshape(1, -1),
        "wq_nope": wq_nope.astype(BF16),
        "wq_pe": wq_pe.reshape(MLA_Q_LORA, -1).astype(BF16),
        "wq_rot": wq_rot.reshape(MLA_Q_LORA, -1).astype(BF16),
        "e_q": jnp.asarray(e_q, BF16),
        "e_k": jnp.asarray(e_k, BF16),
        "wuk_pad": wuk_pad.astype(BF16),
        "wuv": w_uv_l.reshape(MLA_KV_LORA, MLA_W).astype(BF16),
        "wuv_t": w_uv_l.reshape(MLA_KV_LORA, MLA_W).T.astype(BF16),
        "wabs": wabs.astype(BF16),
        "hgrn_norm": hgrn_norm_l.reshape(1, -1),
        "w_bo": w_bo_l.astype(BF16),
        "w_out": w_out_l.astype(BF16),
        "g_ffn": g_ffn_l.reshape(1, d),
        "w_ff1": w_ff1_l.astype(BF16),
        "w_ff2": w_ff2_l.astype(BF16),
    }


def _rope_tables(pos):
    def cs(half):
        inv = ROPE_THETA ** (-jnp.arange(half, dtype=F32) / half)
        ang = pos.astype(F32)[:, None] * inv[None, :]
        return jnp.cos(ang), jnp.sin(ang)

    c_ret, s_ret = cs(RET_DK // 2)
    c_pe, s_pe = cs(MLA_ROPE // 2)
    two = lambda a: jnp.concatenate([a, a], axis=1)
    return {
        "cos_ret": jnp.tile(c_ret, (1, RET_HEADS)), "sin_ret": jnp.tile(s_ret, (1, RET_HEADS)),
        "cos_q": jnp.tile(two(c_pe), (1, MLA_HEADS)), "sin_q": jnp.tile(two(s_pe), (1, MLA_HEADS)),
        "cos_k": two(c_pe), "sin_k": two(s_pe),
    }


def _tiles(t):
    return {"conv": min(512, t), "hgrn": min(256, t), "prep": min(512, t), "tq": min(512, t), "tk": min(512, t)}


def _trunk_layer(h, tabs, ret_state, conv_buf, hgrn_state, lb, lw, *, b, t, attend, final_gain):
    n = b * t
    tl = _tiles(t)
    odt = BF16 if t % 16 == 0 else F32
    proj_a, proj_b = _inproj(h, lw["g_mix"], lw["w_in"], tm=min(n, 2048), adt=odt)
    o_ret, ret_new = _retention(proj_a, proj_b, tabs["cos_ret"], tabs["sin_ret"], ret_state, b=b, t=t, odt=odt)
    o_conv, conv_new = _short_conv(proj_b, lw["conv_w"], conv_buf, b=b, t=t, rows=tl["conv"], odt=odt)
    o_h, hgrn_new = _hgrn(proj_b, lb, lw["hgrn_norm"], hgrn_state, b=b, t=t, rows=tl["hgrn"], odt=odt)
    o_mla, ckv, kpe = attend(proj_b, lw, tabs, tl, odt)
    h = _merge(h, proj_a, (o_ret, o_conv, o_mla, o_h), lw["w_bo"], lw["w_out"], tm=min(n, 512))
    gf = final_gain if final_gain is not None else lw["g_ffn"]
    h = _ffn(h, lw["g_ffn"], lw["w_ff1"], lw["w_ff2"], gf, tm=min(n, 512), tf=512,
             final_norm=final_gain is not None)
    return h, ret_new, conv_new, hgrn_new, ckv, kpe


def kernel(x_prompt, x_sample, cache_mla_latent, cache_mla_rope, page_table,
           state_retention, state_conv, state_hgrn,
           norm_mix, w_in, conv_w, mla_q_norm, mla_w_uq, mla_kv_norm, mla_w_uk, mla_w_uv,
           hgrn_lb_logits, hgrn_out_norm, w_branch_out, w_out, norm_ffn, w_ff1, w_ff2, norm_final):
    bp, tp, d = x_prompt.shape
    bs, ts, _ = x_sample.shape
    depth = w_in.shape[0]
    past_len = page_table.shape[1] * PAGE_SIZE
    tabs_p = _rope_tables(jnp.arange(tp, dtype=jnp.int32))
    tabs_s = _rope_tables(past_len + jnp.arange(ts, dtype=jnp.int32))
    lb_p = jax.nn.softmax(hgrn_lb_logits.astype(F32), axis=0)
    lower_bounds = jnp.cumsum(lb_p, axis=0) - lb_p[0:1]
    gfin = norm_final.reshape(1, d)
    rope_pool_t = jnp.swapaxes(cache_mla_rope, 2, 3)

    hp = x_prompt.reshape(bp * tp, d)
    hs = x_sample.reshape(bs * ts, d)
    zeros_ret = jnp.zeros((bp, RET_HEADS, RET_DK, RET_DV), F32)
    zeros_conv = jnp.zeros((bp, CONV_K - 1, CONV_WIDTH), F32)
    zeros_hgrn = jnp.zeros((bp, HGRN_HEADS, HGRN_DK, HGRN_DV), F32)
    outs = {k: [] for k in ("lat_p", "rope_p", "lat_s", "rope_s", "ret_p", "ret_s",
                            "conv_p", "conv_s", "hg_p", "hg_s")}
    for l in range(depth):
        lw = _pack_layer(w_in[l], conv_w[l], mla_q_norm[l], mla_w_uq[l], mla_kv_norm[l], mla_w_uk[l],
                         mla_w_uv[l], hgrn_out_norm[l], w_branch_out[l], w_out[l], norm_mix[l],
                         norm_ffn[l], w_ff1[l], w_ff2[l])
        lb = lower_bounds[l].reshape(1, HGRN_W)
        last = gfin if l == depth - 1 else None

        def attend_prompt(proj, lw, tabs, tl, odt):
            q, ckv, kpe, kc, v = _mla_prep(proj, lw, tabs, n=bp * tp, t=tp, tm=tl["prep"], with_kv=True)
            o = _flash(q, kc, v, b=bp, t=tp, tq=tl["tq"], tk=tl["tk"], odt=odt)
            return o, ckv, kpe

        def attend_sample(proj, lw, tabs, tl, odt, layer=l):
            q, ckv, kpe = _mla_prep(proj, lw, tabs, n=bs * ts, t=ts, tm=tl["prep"], with_kv=False)
            o = _paged_attention(page_table, q.reshape(bs, ts, -1), ckv.reshape(bs, ts, -1),
                                 kpe.reshape(bs, ts, -1), lw["wabs"], lw["wuv"],
                                 cache_mla_latent, rope_pool_t, layer=layer,
                                 pg=math.gcd(16, page_table.shape[1]), nb=math.gcd(2, bs))
            return o.reshape(bs * ts, MLA_W), ckv, kpe

        hp, r, c, g, ckv, kpe = _trunk_layer(hp, tabs_p, zeros_ret, zeros_conv, zeros_hgrn, lb, lw,
                                             b=bp, t=tp, attend=attend_prompt, final_gain=last)
        outs["ret_p"].append(r); outs["conv_p"].append(c); outs["hg_p"].append(g)
        outs["lat_p"].append(ckv.reshape(bp, tp, -1)); outs["rope_p"].append(kpe.reshape(bp, tp, -1))
        hs, r, c, g, ckv, kpe = _trunk_layer(hs, tabs_s, state_retention[l], state_conv[l], state_hgrn[l],
                                             lb, lw, b=bs, t=ts, attend=attend_sample, final_gain=last)
        outs["ret_s"].append(r); outs["conv_s"].append(c); outs["hg_s"].append(g)
        outs["lat_s"].append(ckv.reshape(bs, ts, -1)); outs["rope_s"].append(kpe.reshape(bs, ts, -1))

    st = lambda k: jnp.stack(outs[k])
    return (hp.reshape(bp, tp, d), hs.reshape(bs, ts, d),
            st("lat_p"), st("rope_p"), st("lat_s"), st("rope_s"),
            st("ret_p"), st("ret_s"), st("conv_p"), st("conv_s"), st("hg_p"), st("hg_s"))
```

```python
import functools
import math

import numpy as np
import jax
import jax.numpy as jnp
from jax import lax
from jax.experimental import pallas as pl
from jax.experimental.pallas import tpu as pltpu

F32 = jnp.float32
BF16 = jnp.bfloat16

D_MODEL = 1024
PAGE_SIZE = 128
RET_HEADS, RET_DK, RET_DV, RET_CHUNK = 4, 64, 128, 128
CONV_WIDTH, CONV_K = 512, 3
MLA_HEADS, MLA_Q_LORA, MLA_KV_LORA, MLA_NOPE, MLA_ROPE, MLA_V = 8, 384, 256, 64, 32, 64
MLA_SCALE = (MLA_NOPE + MLA_ROPE) ** -0.5
HGRN_HEADS, HGRN_DK, HGRN_DV = 4, 128, 128
D_FF = 4 * D_MODEL
ROPE_THETA = 10000.0
NORM_EPS = 1e-6
N_BRANCH = 4
NEG_BIG = -1e30
LOG2E = math.log2(math.e)

RET_W = RET_HEADS * RET_DV
MLA_W = MLA_HEADS * MLA_V
HGRN_W = HGRN_HEADS * HGRN_DV
IN_SIZES = (
    RET_HEADS * RET_DK, RET_HEADS * RET_DK, RET_W, RET_W,
    CONV_WIDTH, CONV_WIDTH, CONV_WIDTH,
    MLA_Q_LORA, MLA_KV_LORA, MLA_ROPE,
    HGRN_HEADS * HGRN_DK, HGRN_HEADS * HGRN_DK, HGRN_W, HGRN_W,
    N_BRANCH * D_MODEL,
)

LANES = 128
SUBLANES = 8
VMEM_LIMIT_BYTES = 56 * 1024 * 1024

IN_TILE = 768
GATE_W = N_BRANCH * D_MODEL
RV_OFF = GATE_W
PROJ_A_W = GATE_W + RET_W
HG_BLOCK_W = 4 * HGRN_W
RET_QKG_W = 1024
RET_QKG_OFF = HG_BLOCK_W
CONV_BLOCK_W = 3 * CONV_WIDTH
CONV_OFF = RET_QKG_OFF + RET_QKG_W
MLA_BLOCK_W = 768
MLA_OFF = CONV_OFF + CONV_BLOCK_W
PROJ_B_W = MLA_OFF + MLA_BLOCK_W
MLA_HEAD_PAD = 128
HGRN_SUB = 16
HGRN_CHUNK = 64
HGRN_SAFE_LOG_DECAY = 160.0
KEY_PAD = 128


def _cparams(sem):
    return pltpu.CompilerParams(dimension_semantics=sem, vmem_limit_bytes=VMEM_LIMIT_BYTES)


def _rms(x, g):
    return x * lax.rsqrt(jnp.mean(x * x, axis=-1, keepdims=True) + NORM_EPS) * g


def _silu(x):
    return x * jax.nn.sigmoid(x)


def _dot(a, b):
    return jnp.dot(a, b, preferred_element_type=F32)


def _dot_nt(a, b):
    return lax.dot_general(a, b, (((1,), (1,)), ((), ())), preferred_element_type=F32)


def _dot_tn(a, b):
    return lax.dot_general(a, b, (((0,), (0,)), ((), ())), preferred_element_type=F32)


def _inproj_kernel(x_ref, g_ref, w_ref, oa_ref, ob_ref, xn_ref, *, na):
    j = pl.program_id(1)

    @pl.when(j == 0)
    def _():
        xn_ref[...] = _rms(x_ref[...], g_ref[...]).astype(BF16)

    @pl.when(j < na)
    def _():
        oa_ref[...] = _dot(xn_ref[...], w_ref[...]).astype(oa_ref.dtype)

    @pl.when(j >= na)
    def _():
        ob_ref[...] = _dot(xn_ref[...], w_ref[...])


def _inproj(x, g, w, *, tm, adt):
    n, d = x.shape
    tn = IN_TILE
    na = PROJ_A_W // tn
    nb = PROJ_B_W // tn
    return pl.pallas_call(
        functools.partial(_inproj_kernel, na=na),
        out_shape=(jax.ShapeDtypeStruct((n, PROJ_A_W), adt), jax.ShapeDtypeStruct((n, PROJ_B_W), F32)),
        grid=(n // tm, na + nb),
        in_specs=[
            pl.BlockSpec((tm, d), lambda i, j: (i, 0)),
            pl.BlockSpec((1, d), lambda i, j: (0, 0)),
            pl.BlockSpec((d, tn), lambda i, j: (0, j)),
        ],
        out_specs=(pl.BlockSpec((tm, tn), lambda i, j: (i, jnp.minimum(j, na - 1))),
                   pl.BlockSpec((tm, tn), lambda i, j: (i, jnp.maximum(j - na, 0)))),
        scratch_shapes=[pltpu.VMEM((tm, d), BF16)],
        compiler_params=_cparams(("parallel", "arbitrary")),
        name="in_proj",
    )(x, g, w)


def _ret_kernel(x_ref, v_ref, cos_ref, sin_ref, decay_ref, qdec_ref, kdec_ref, s0_ref,
                o_ref, sout_ref, s_ref, *, nt, cdec, c):
    t = pl.program_id(1)
    hw = RET_HEADS * RET_DK // 2
    per = RET_DK // 2

    @pl.when(t == 0)
    def _():
        s_ref[...] = jnp.zeros(s_ref.shape, F32)
        for h in range(RET_HEADS):
            s_ref[h, h * per:(h + 1) * per, :] = s0_ref[0, h, 0:per, :]
            s_ref[h, hw + h * per:hw + (h + 1) * per, :] = s0_ref[0, h, per:2 * per, :]

    lane = lax.broadcasted_iota(jnp.int32, (1, 2 * hw), 1)
    head_of_lane = (lane % hw) // per
    for sub in range(x_ref.shape[0] // c):
        r0 = sub * c
        x = x_ref[r0:r0 + c, :]
        cos = cos_ref[r0:r0 + c, :]
        sin = sin_ref[r0:r0 + c, :]
        q1, q2 = x[:, 0:hw], x[:, hw:2 * hw]
        k1, k2 = x[:, 2 * hw:3 * hw], x[:, 3 * hw:4 * hw]
        qr = jnp.concatenate([q1 * cos - q2 * sin, q1 * sin + q2 * cos], axis=1)
        kr = jnp.concatenate([k1 * cos - k2 * sin, k1 * sin + k2 * cos], axis=1) * (RET_DK ** -0.5)
        v = v_ref[r0:r0 + c, :]
        g = x[:, 4 * hw:4 * hw + RET_W]
        if c < KEY_PAD:
            kr = jnp.concatenate([kr, jnp.zeros((KEY_PAD - c, kr.shape[1]), F32)], axis=0)
            v = jnp.concatenate([v, jnp.zeros((KEY_PAD - c, v.shape[1]), v.dtype)], axis=0)
        kt = kr.T
        ktb = kt.astype(BF16)
        qms = [jnp.where(head_of_lane == h, qr, 0.0).astype(BF16) for h in range(RET_HEADS)]
        atts = [(_dot(qms[h], ktb) * decay_ref[h]).astype(BF16) for h in range(RET_HEADS)]
        for h in range(RET_HEADS):
            qm = qms[h]
            vh = v[:, h * RET_DV:(h + 1) * RET_DV].astype(BF16)
            sh = s_ref[h]
            o = _dot(atts[h], vh) + _dot(qm, sh.astype(BF16)) * qdec_ref[h]
            s_ref[h] = sh * cdec[h] + _dot((kt * kdec_ref[h]).astype(BF16), vh)
            o = o * lax.rsqrt(jnp.mean(o * o, axis=-1, keepdims=True) + NORM_EPS)
            gh = g[:, h * RET_DV:(h + 1) * RET_DV]
            o_ref[r0:r0 + c, h * RET_DV:(h + 1) * RET_DV] = (o * _silu(gh)).astype(o_ref.dtype)

    @pl.when(t == nt - 1)
    def _():
        for h in range(RET_HEADS):
            sout_ref[0, h, 0:per, :] = s_ref[h, h * per:(h + 1) * per, :]
            sout_ref[0, h, per:2 * per, :] = s_ref[h, hw + h * per:hw + (h + 1) * per, :]


def _retention(proj_a, proj_b, cos, sin, state, *, b, t, odt):
    c = math.gcd(t, RET_CHUNK)
    rows = c * math.gcd(t // c, 2)
    nt = t // rows
    ck = max(c, KEY_PAD)
    hw = RET_HEADS * RET_DK // 2
    hs = np.arange(RET_HEADS, dtype=np.float64)
    log_gamma = np.log1p(-np.exp2(-5.0 - hs))
    j = np.arange(c, dtype=np.float64)
    rel = j[:, None] - j[None, :]
    decay = np.where(rel >= 0, np.exp(log_gamma[:, None, None] * np.where(rel >= 0, rel, 0.0)), 0.0)
    decay_p = np.zeros((RET_HEADS, c, ck))
    decay_p[:, :, :c] = decay
    qdec = np.repeat(np.exp(log_gamma[:, None] * (j + 1.0))[:, :, None], RET_DV, axis=2)
    kdec = np.zeros((RET_HEADS, 1, ck))
    kdec[:, 0, :c] = np.exp(log_gamma[:, None] * (c - 1.0 - j))
    cdec = tuple(float(np.float32(np.exp(lg * c))) for lg in log_gamma)
    kern = functools.partial(_ret_kernel, nt=nt, cdec=cdec, c=c)
    return pl.pallas_call(
        kern,
        out_shape=(jax.ShapeDtypeStruct((b * t, RET_W), odt),
                   jax.ShapeDtypeStruct((b, RET_HEADS, RET_DK, RET_DV), F32)),
        grid=(b, nt),
        in_specs=[
            pl.BlockSpec((rows, RET_QKG_W), lambda i, k: (i * nt + k, RET_QKG_OFF // RET_QKG_W)),
            pl.BlockSpec((rows, RET_W), lambda i, k: (i * nt + k, RV_OFF // RET_W)),
            pl.BlockSpec((rows, hw), lambda i, k: (k, 0)),
            pl.BlockSpec((rows, hw), lambda i, k: (k, 0)),
            pl.BlockSpec((RET_HEADS, c, ck), lambda i, k: (0, 0, 0)),
            pl.BlockSpec((RET_HEADS, c, RET_DV), lambda i, k: (0, 0, 0)),
            pl.BlockSpec((RET_HEADS, 1, ck), lambda i, k: (0, 0, 0)),
            pl.BlockSpec((1, RET_HEADS, RET_DK, RET_DV), lambda i, k: (i, 0, 0, 0)),
        ],
        out_specs=(pl.BlockSpec((rows, RET_W), lambda i, k: (i * nt + k, 0)),
                   pl.BlockSpec((1, RET_HEADS, RET_DK, RET_DV), lambda i, k: (i, 0, 0, 0))),
        scratch_shapes=[pltpu.VMEM((RET_HEADS, RET_HEADS * RET_DK, RET_DV), F32)],
        compiler_params=_cparams(("parallel", "arbitrary")),
        name="retention",
    )(proj_b, proj_a, cos, sin, jnp.asarray(decay_p, F32), jnp.asarray(qdec, F32), jnp.asarray(kdec, F32), state)


def _conv_kernel(x_ref, w_ref, buf_ref, o_ref, bout_ref, carry_ref, *, nt):
    t = pl.program_id(1)
    w = CONV_WIDTH

    @pl.when(t == 0)
    def _():
        carry_ref[...] = jnp.zeros(carry_ref.shape, F32)
        carry_ref[SUBLANES - 2:SUBLANES, :] = buf_ref[0]

    x = x_ref[...]
    rows = x.shape[0]
    gate_b, gate_c, xin = x[:, 0:w], x[:, w:2 * w], x[:, 2 * w:3 * w]
    u = gate_c * xin
    row = lax.broadcasted_iota(jnp.int32, (rows, 1), 0)
    p1 = carry_ref[SUBLANES - 1:SUBLANES, :]
    p2 = carry_ref[SUBLANES - 2:SUBLANES - 1, :]
    u1 = jnp.where(row == 0, p1, pltpu.roll(u, 1, 0))
    u2 = jnp.where(row == 0, p2, jnp.where(row == 1, p1, pltpu.roll(u, 2, 0)))
    y = u2 * w_ref[0:1, :] + u1 * w_ref[1:2, :] + u * w_ref[2:3, :]
    o_ref[...] = (gate_b * y).astype(o_ref.dtype)
    carry_ref[...] = u[rows - SUBLANES:rows, :]

    @pl.when(t == nt - 1)
    def _():
        bout_ref[0] = u[rows - 2:rows, :]


def _short_conv(proj, w, buf, *, b, t, rows, odt):
    nt = t // rows
    kern = functools.partial(_conv_kernel, nt=nt)
    return pl.pallas_call(
        kern,
        out_shape=(jax.ShapeDtypeStruct((b * t, CONV_WIDTH), odt),
                   jax.ShapeDtypeStruct((b, CONV_K - 1, CONV_WIDTH), F32)),
        grid=(b, nt),
        in_specs=[
            pl.BlockSpec((rows, CONV_BLOCK_W), lambda i, k: (i * nt + k, CONV_OFF // CONV_BLOCK_W)),
            pl.BlockSpec((CONV_K, CONV_WIDTH), lambda i, k: (0, 0)),
            pl.BlockSpec((1, CONV_K - 1, CONV_WIDTH), lambda i, k: (i, 0, 0)),
        ],
        out_specs=(pl.BlockSpec((rows, CONV_WIDTH), lambda i, k: (i * nt + k, 0)),
                   pl.BlockSpec((1, CONV_K - 1, CONV_WIDTH), lambda i, k: (i, 0, 0))),
        scratch_shapes=[pltpu.VMEM((SUBLANES, CONV_WIDTH), F32)],
        compiler_params=_cparams(("parallel", "arbitrary")),
        name="short_conv",
    )(proj, w, buf)


def _split3(x):
    hi = x.astype(BF16)
    r = x - hi.astype(F32)
    mid = r.astype(BF16)
    lo = (r - mid.astype(F32)).astype(BF16)
    return hi, mid, lo


def _hgrn_intra_exact(gcum, qh, kk, hv, g_ref, k_ref, oi_ref):
    ell = gcum.shape[0]
    dk = HGRN_DK
    g_ref[...] = gcum
    k_ref[...] = kk
    nsub = ell // HGRN_SUB
    lane = lax.broadcasted_iota(jnp.int32, (1, KEY_PAD), 1)
    key_row = lax.broadcasted_iota(jnp.int32, (ell, 1), 0)
    sub_row = lax.broadcasted_iota(jnp.int32, (HGRN_SUB, 1), 0)
    a_blocks = [[None] * nsub for _ in range(HGRN_HEADS)]
    for j in range(nsub):
        r0 = j * HGRN_SUB
        gq = gcum[r0:r0 + HGRN_SUB, :]
        qq = qh[r0:r0 + HGRN_SUB, :]

        def diag_body(s, carry, r0=r0, gq=gq, qq=qq):
            gs = g_ref[pl.ds(r0 + s, 1), :]
            ks = k_ref[pl.ds(r0 + s, 1), :]
            valid = sub_row >= s
            p = jnp.where(valid, jnp.exp(jnp.minimum(gq - gs, 0.0)) * qq * ks, 0.0)
            out = []
            for h in range(HGRN_HEADS):
                a = jnp.sum(p[:, h * dk:(h + 1) * dk], axis=-1, keepdims=True)
                out.append(carry[h] + jnp.where(lane == r0 + s, a, 0.0))
            return tuple(out)

        init = tuple(jnp.zeros((HGRN_SUB, KEY_PAD), F32) for _ in range(HGRN_HEADS))
        diag = lax.fori_loop(0, HGRN_SUB, diag_body, init)
        if j > 0:
            gb = gcum[r0 - 1:r0, :]
            q_t = (qq * jnp.exp(gq - gb)).astype(BF16)
            k_t = jnp.where(key_row < r0, kk * jnp.exp(jnp.minimum(gb - gcum, 0.0)), 0.0)
            if ell < KEY_PAD:
                k_t = jnp.concatenate([k_t, jnp.zeros((KEY_PAD - ell, k_t.shape[1]), F32)], axis=0)
            k_t = k_t.astype(BF16)
        for h in range(HGRN_HEADS):
            blk = diag[h]
            if j > 0:
                blk = blk + _dot_nt(q_t[:, h * dk:(h + 1) * dk], k_t[:, h * dk:(h + 1) * dk])
            a_blocks[h][j] = blk
    for h in range(HGRN_HEADS):
        a_h = a_blocks[h][0] if nsub == 1 else jnp.concatenate(a_blocks[h], axis=0)
        vh = hv[:, h * HGRN_DV:(h + 1) * HGRN_DV]
        if ell < KEY_PAD:
            vh = jnp.concatenate([vh, jnp.zeros((KEY_PAD - ell, HGRN_DV), F32)], axis=0)
        oi_ref[h] = _dot(a_h.astype(BF16), vh.astype(BF16))


def _hgrn_intra_fast(parts, oi_ref):
    dk = HGRN_DK
    ell = parts[0][0].shape[0]
    causal = lax.broadcasted_iota(jnp.int32, (ell, ell), 0) >= lax.broadcasted_iota(jnp.int32, (ell, ell), 1)
    scores = []
    for gcum, g_end, qh, kk, _, _, _ in parts:
        gm = 0.5 * g_end
        q_t = (qh * jnp.exp(gcum - gm)).astype(BF16)
        k_t = (kk * jnp.exp(gm - gcum)).astype(BF16)
        scores.append([_dot_nt(q_t[:, h * dk:(h + 1) * dk], k_t[:, h * dk:(h + 1) * dk])
                       for h in range(HGRN_HEADS)])
    masked = [[jnp.where(causal, a, 0.0).astype(BF16) for a in row] for row in scores]
    for c, prt in enumerate(parts):
        hv = prt[4]
        for h in range(HGRN_HEADS):
            oi_ref[c, h] = _dot(masked[c][h], hv[:, h * HGRN_DV:(h + 1) * HGRN_DV].astype(BF16))


def _hgrn_kernel(x_ref, lb_ref, gn_ref, s0_ref, o_ref, sout_ref,
                 st_ref, g_ref, k_ref, oi_ref, *, nt, chunk):
    t = pl.program_id(1)
    w = HGRN_W
    dk = HGRN_DK

    @pl.when(t == 0)
    def _():
        for h in range(HGRN_HEADS):
            st_ref[h] = s0_ref[0, h].T

    rows = x_ref.shape[0]
    lb = lb_ref[...]
    gn = gn_ref[...]
    ri = lax.broadcasted_iota(jnp.int32, (chunk, chunk), 0)
    ci = lax.broadcasted_iota(jnp.int32, (chunk, chunk), 1)
    tri = jnp.where(ri >= ci, 1.0, 0.0).astype(BF16)
    parts = []
    for c in range(max(rows // chunk, 1)):
        if rows >= chunk:
            x = x_ref[c * chunk:(c + 1) * chunk, :]
            live = chunk
        else:
            x = jnp.concatenate([x_ref[...], jnp.zeros((chunk - rows, x_ref.shape[1]), F32)], axis=0)
            live = rows
        hq, hf, hv, hg = x[:, 0:w], x[:, w:2 * w], x[:, 2 * w:3 * w], x[:, 3 * w:4 * w]
        f_gate = lb + (1.0 - lb) * jax.nn.sigmoid(hf)
        log_f = jnp.log(f_gate)
        kk = 1.0 - f_gate
        if live < chunk:
            ok = lax.broadcasted_iota(jnp.int32, (chunk, 1), 0) < live
            log_f = jnp.where(ok, log_f, 0.0)
            kk = jnp.where(ok, kk, 0.0)
        qh = _silu(hq)
        p_hi, p_mid, p_lo = _split3(log_f)
        gcum = _dot(tri, p_hi) + _dot(tri, p_mid) + _dot(tri, p_lo)
        parts.append((gcum, gcum[chunk - 1:chunk, :], qh, kk, hv, hg, live))

    g_min = parts[0][1]
    for prt in parts[1:]:
        g_min = jnp.minimum(g_min, prt[1])
    safe = jnp.min(g_min) > -HGRN_SAFE_LOG_DECAY

    @pl.when(safe)
    def _():
        _hgrn_intra_fast(parts, oi_ref)

    @pl.when(jnp.logical_not(safe))
    def _():
        for c, (gcum, g_end, qh, kk, hv, _, _) in enumerate(parts):
            _hgrn_intra_exact(gcum, qh, kk, hv, g_ref, k_ref, oi_ref.at[c])

    for c, (gcum, g_end, qh, kk, hv, hg, live) in enumerate(parts):
        q_in = (qh * jnp.exp(gcum)).astype(BF16)
        k_out = (kk * jnp.exp(g_end - gcum)).astype(BF16)
        e_end = jnp.exp(g_end)
        for h in range(HGRN_HEADS):
            sl = slice(h * dk, (h + 1) * dk)
            st = st_ref[h]
            o = oi_ref[c, h] + _dot_nt(q_in[:, sl], st.astype(BF16))
            st_ref[h] = st * e_end[:, sl] + _dot_tn(hv[:, h * HGRN_DV:(h + 1) * HGRN_DV].astype(BF16), k_out[:, sl])
            o = _rms(o, gn) * _silu(hg[:, h * HGRN_DV:(h + 1) * HGRN_DV])
            o_ref[c * chunk:c * chunk + live, h * HGRN_DV:(h + 1) * HGRN_DV] = o[0:live, :].astype(o_ref.dtype)

    @pl.when(t == nt - 1)
    def _():
        for h in range(HGRN_HEADS):
            sout_ref[0, h] = st_ref[h].T


def _hgrn(proj, lb, gnorm, state, *, b, t, rows, odt):
    nt = t // rows
    chunk = max(min(rows, HGRN_CHUNK), HGRN_SUB)
    nchunk = max(rows // chunk, 1)
    kern = functools.partial(_hgrn_kernel, nt=nt, chunk=chunk)
    return pl.pallas_call(
        kern,
        out_shape=(jax.ShapeDtypeStruct((b * t, HGRN_W), odt),
                   jax.ShapeDtypeStruct((b, HGRN_HEADS, HGRN_DK, HGRN_DV), F32)),
        grid=(b, nt),
        in_specs=[
            pl.BlockSpec((rows, HG_BLOCK_W), lambda i, k: (i * nt + k, 0)),
            pl.BlockSpec((1, HGRN_W), lambda i, k: (0, 0)),
            pl.BlockSpec((1, HGRN_DV), lambda i, k: (0, 0)),
            pl.BlockSpec((1, HGRN_HEADS, HGRN_DK, HGRN_DV), lambda i, k: (i, 0, 0, 0)),
        ],
        out_specs=(pl.BlockSpec((rows, HGRN_W), lambda i, k: (i * nt + k, 0)),
                   pl.BlockSpec((1, HGRN_HEADS, HGRN_DK, HGRN_DV), lambda i, k: (i, 0, 0, 0))),
        scratch_shapes=[pltpu.VMEM((HGRN_HEADS, HGRN_DV, HGRN_DK), F32),
                        pltpu.VMEM((chunk, HGRN_W), F32),
                        pltpu.VMEM((chunk, HGRN_W), F32),
                        pltpu.VMEM((nchunk, HGRN_HEADS, chunk, HGRN_DV), F32)],
        compiler_params=_cparams(("parallel", "arbitrary")),
        name="hgrn2",
    )(proj, lb, gnorm, state)


def _mla_prep_kernel(x_ref, qn_ref, kvn_ref, wqn_ref, wqp_ref, wqr_ref, eq_ref,
                     cq_ref, sq_ref, ck_ref, sk_ref, *rest, with_kv):
    if with_kv:
        wuk_ref, ek_ref, wuv_ref, q_ref, lat_ref, kpe_ref, kc_ref, v_ref = rest
    else:
        q_ref, lat_ref, kpe_ref = rest
    x = x_ref[...]
    mcq = x[:, 0:MLA_Q_LORA]
    mckv = x[:, MLA_Q_LORA:MLA_Q_LORA + MLA_KV_LORA]
    o = MLA_Q_LORA + MLA_KV_LORA
    mkr = x[:, o:o + MLA_ROPE]
    mkr_rot = x[:, o + MLA_ROPE:o + 2 * MLA_ROPE]
    xq = _rms(mcq, qn_ref[...]).astype(BF16)
    q_pe = _dot(xq, wqp_ref[...]) * cq_ref[...] + _dot(xq, wqr_ref[...]) * sq_ref[...]
    q_cat = _dot(xq, wqn_ref[...]) + _dot(q_pe.astype(BF16), eq_ref[...])
    q_ref[...] = (q_cat * (MLA_SCALE * LOG2E)).astype(q_ref.dtype)
    ckv = _rms(mckv, kvn_ref[...])
    lat_ref[...] = ckv
    kpe = mkr * ck_ref[...] + mkr_rot * sk_ref[...]
    kpe_ref[...] = kpe
    if with_kv:
        cb = ckv.astype(BF16)
        kc_ref[...] = (_dot(cb, wuk_ref[...]) + _dot(kpe.astype(BF16), ek_ref[...])).astype(BF16)
        v_ref[0] = _dot_nt(wuv_ref[...], cb).astype(BF16)


def _mla_prep(proj, lw, tabs, *, n, t, tm, with_kv):
    rows = min(tm, t)
    nt = t // rows
    tables = [tabs["cos_q"], tabs["sin_q"], tabs["cos_k"], tabs["sin_k"]]
    if t < tm and not with_kv:
        reps = math.gcd(tm // t, n // t)
        rows = t * reps
        tables = [jnp.tile(a, (reps, 1)) for a in tables]
    hp = MLA_HEADS * MLA_HEAD_PAD
    qp_w = MLA_HEADS * MLA_ROPE
    full = lambda r, c: pl.BlockSpec((r, c), lambda i: (0, 0))
    tab = lambda c: pl.BlockSpec((rows, c), lambda i: (i % nt, 0))
    in_specs = [
        pl.BlockSpec((rows, MLA_BLOCK_W), lambda i: (i, MLA_OFF // MLA_BLOCK_W)),
        full(1, MLA_Q_LORA), full(1, MLA_KV_LORA),
        full(MLA_Q_LORA, hp), full(MLA_Q_LORA, qp_w), full(MLA_Q_LORA, qp_w), full(qp_w, hp),
        tab(qp_w), tab(qp_w), tab(MLA_ROPE), tab(MLA_ROPE),
    ]
    args = [proj, lw["q_norm"], lw["kv_norm"], lw["wq_nope"], lw["wq_pe"], lw["wq_rot"], lw["e_q"]] + tables
    q_dtype = BF16 if with_kv else F32
    out_shape = [jax.ShapeDtypeStruct((n, hp), q_dtype),
                 jax.ShapeDtypeStruct((n, MLA_KV_LORA), F32),
                 jax.ShapeDtypeStruct((n, MLA_ROPE), F32)]
    out_specs = [pl.BlockSpec((rows, hp), lambda i: (i, 0)),
                 pl.BlockSpec((rows, MLA_KV_LORA), lambda i: (i, 0)),
                 pl.BlockSpec((rows, MLA_ROPE), lambda i: (i, 0))]
    if with_kv:
        in_specs += [full(MLA_KV_LORA, hp), full(MLA_ROPE, hp), full(MLA_W, MLA_KV_LORA)]
        args += [lw["wuk_pad"], lw["e_k"], lw["wuv_t"]]
        out_shape += [jax.ShapeDtypeStruct((n, hp), BF16), jax.ShapeDtypeStruct((n // t, MLA_W, t), BF16)]
        out_specs += [pl.BlockSpec((rows, hp), lambda i: (i, 0)),
                      pl.BlockSpec((1, MLA_W, rows), lambda i: (i // nt, 0, i % nt))]
    return pl.pallas_call(
        functools.partial(_mla_prep_kernel, with_kv=with_kv),
        out_shape=tuple(out_shape),
        grid=(n // rows,),
        in_specs=in_specs,
        out_specs=tuple(out_specs),
        compiler_params=_cparams(("parallel",)),
        name="mla_prep",
    )(*args)


def _flash_kernel(q_ref, k_ref, vt_ref, o_ref, m_ref, l_ref, acc_ref, *, tq, tk):
    qi = pl.program_id(1)
    kj = pl.program_id(2)
    hp = MLA_HEAD_PAD

    @pl.when(kj == 0)
    def _():
        m_ref[...] = jnp.full(m_ref.shape, NEG_BIG, F32)
        l_ref[...] = jnp.zeros(l_ref.shape, F32)
        acc_ref[...] = jnp.zeros(acc_ref.shape, F32)

    def scores(h):
        return _dot_nt(k_ref[:, h * hp:(h + 1) * hp], q_ref[:, h * hp:(h + 1) * hp])

    def step(mask):
        ahead = 2
        pending = [scores(h) for h in range(ahead)]
        for h in range(MLA_HEADS):
            s = pending.pop(0)
            if h + ahead < MLA_HEADS:
                pending.append(scores(h + ahead))
            if mask is not None:
                s = jnp.where(mask, s, NEG_BIG)
            m_old = m_ref[h]
            m_new = jnp.maximum(m_old, jnp.max(s, axis=0, keepdims=True))
            alpha = jnp.exp2(m_old - m_new)
            p = jnp.exp2(s - m_new)
            l_ref[h] = l_ref[h] * alpha + jnp.sum(p, axis=0, keepdims=True)
            vt = vt_ref[0, h * MLA_V:(h + 1) * MLA_V, :]
            acc_ref[h] = acc_ref[h] * alpha + _dot(vt, p.astype(BF16))
            m_ref[h] = m_new

    @pl.when(kj < qi)
    def _():
        step(None)

    @pl.when(kj == qi)
    def _():
        step(lax.broadcasted_iota(jnp.int32, (tk, tq), 0) <= lax.broadcasted_iota(jnp.int32, (tk, tq), 1))
        for h in range(MLA_HEADS):
            acc_ref[h] = acc_ref[h] / l_ref[h]
        o_ref[...] = acc_ref[...].reshape(MLA_W, tq).T.astype(o_ref.dtype)


def _flash(q, k, vt, *, b, t, tq, tk, odt):
    nq, nk = t // tq, t // tk
    hp = MLA_HEADS * MLA_HEAD_PAD
    kern = functools.partial(_flash_kernel, tq=tq, tk=tk)
    return pl.pallas_call(
        kern,
        out_shape=jax.ShapeDtypeStruct((b * t, MLA_W), odt),
        grid=(b, nq, nk),
        in_specs=[
            pl.BlockSpec((tq, hp), lambda i, a, c: (i * nq + a, 0)),
            pl.BlockSpec((tk, hp), lambda i, a, c: (i * nk + jnp.minimum(c, a), 0)),
            pl.BlockSpec((1, MLA_W, tk), lambda i, a, c: (i, 0, jnp.minimum(c, a))),
        ],
        out_specs=pl.BlockSpec((tq, MLA_W), lambda i, a, c: (i * nq + a, 0)),
        scratch_shapes=[pltpu.VMEM((MLA_HEADS, 1, tq), F32),
                        pltpu.VMEM((MLA_HEADS, 1, tq), F32),
                        pltpu.VMEM((MLA_HEADS, MLA_V, tq), F32)],
        compiler_params=_cparams(("parallel", "parallel", "arbitrary")),
        name="mla_flash",
    )(q, k, vt)


def _paged_kernel(pt_ref, q_ref, lat_new_ref, kpe_new_ref, wabs_ref, wuv_ref, *rest,
                  ng, pg, t, nb):
    npg = nb * pg
    lat_refs = rest[0:npg]
    pe_refs = rest[npg:2 * npg]
    o_ref = rest[2 * npg]
    ql_ref, qp_ref, m_ref, l_ref, acc_ref, c_ref, rt_ref = rest[2 * npg + 1:]
    g = pl.program_id(1)
    hp = MLA_HEAD_PAD
    rows = MLA_HEADS * t

    @pl.when(g == 0)
    def _():
        lane_head = lax.broadcasted_iota(jnp.int32, (1, MLA_HEADS * hp), 1) // hp
        for e in range(nb):
            q = q_ref[e]
            q_exp = jnp.concatenate(
                [jnp.where(lane_head == h, q, 0.0) for h in range(MLA_HEADS)], axis=0).astype(BF16)
            q_abs = _dot(q_exp, wabs_ref[...])
            ql_ref[e] = q_abs[:, 0:MLA_KV_LORA].astype(BF16)
            qp_ref[e] = q_abs[:, MLA_KV_LORA:MLA_KV_LORA + LANES].astype(BF16)
        m_ref[...] = jnp.full(m_ref.shape, NEG_BIG, F32)
        l_ref[...] = jnp.zeros(l_ref.shape, F32)
        acc_ref[...] = jnp.zeros(acc_ref.shape, F32)

    def update(e, s, c):
        m_old = m_ref[e]
        m_new = jnp.maximum(m_old, jnp.max(s, axis=-1, keepdims=True))
        alpha = jnp.exp2(m_old - m_new)
        p = jnp.exp2(s - m_new)
        l_ref[e] = l_ref[e] * alpha + jnp.sum(p, axis=-1, keepdims=True)
        acc_ref[e] = acc_ref[e] * alpha + _dot(p.astype(BF16), c)
        m_ref[e] = m_new

    for e in range(nb):
        for i in range(pg):
            c_ref[e, i * PAGE_SIZE:(i + 1) * PAGE_SIZE, :] = lat_refs[e * pg + i][0, 0].astype(BF16)
            rt_ref[e, :, i * PAGE_SIZE:(i + 1) * PAGE_SIZE] = pe_refs[e * pg + i][0, 0].astype(BF16)
    s_all = [_dot_nt(ql_ref[e], c_ref[e]) + _dot(qp_ref[e, :, 0:MLA_ROPE], rt_ref[e]) for e in range(nb)]
    for e in range(nb):
        update(e, s_all[e], c_ref[e])

    @pl.when(g == ng - 1)
    def _():
        qtok = lax.broadcasted_iota(jnp.int32, (rows, KEY_PAD), 0) % t
        kidx = lax.broadcasted_iota(jnp.int32, (rows, KEY_PAD), 1)
        lane_head = lax.broadcasted_iota(jnp.int32, (1, MLA_W), 1) // MLA_V
        for e in range(nb):
            cn = jnp.concatenate([lat_new_ref[e], jnp.zeros((KEY_PAD - t, MLA_KV_LORA), F32)], axis=0).astype(BF16)
            rn = jnp.concatenate([kpe_new_ref[e], jnp.zeros((KEY_PAD - t, MLA_ROPE), F32)], axis=0).astype(BF16)
            s = _dot_nt(ql_ref[e], cn) + _dot_nt(qp_ref[e, :, 0:MLA_ROPE], rn)
            update(e, jnp.where(kidx <= qtok, s, NEG_BIG), cn)
            o_lat = acc_ref[e] / l_ref[e]
            res = _dot(o_lat.astype(BF16), wuv_ref[...])
            out = jnp.zeros((t, MLA_W), F32)
            for h in range(MLA_HEADS):
                out = out + jnp.where(lane_head == h, res[h * t:(h + 1) * t, :], 0.0)
            o_ref[e] = out


def _paged_attention(page_table, q, lat_new, kpe_new, wabs, wuv, lat_pool, pe_pool, *, layer, pg, nb):
    b, t, hp = q.shape
    n_pages = page_table.shape[1]
    ng = n_pages // pg
    rows = MLA_HEADS * t

    def page_spec(r, c, e, i):
        return pl.BlockSpec((1, 1, r, c), lambda bi, g, pt: (layer, pt[bi * nb + e, g * pg + i], 0, 0))

    in_specs = [
        pl.BlockSpec((nb, t, hp), lambda bi, g, pt: (bi, 0, 0)),
        pl.BlockSpec((nb, t, MLA_KV_LORA), lambda bi, g, pt: (bi, 0, 0)),
        pl.BlockSpec((nb, t, MLA_ROPE), lambda bi, g, pt: (bi, 0, 0)),
        pl.BlockSpec(wabs.shape, lambda bi, g, pt: (0, 0)),
        pl.BlockSpec(wuv.shape, lambda bi, g, pt: (0, 0)),
    ]
    in_specs += [page_spec(PAGE_SIZE, MLA_KV_LORA, e, i) for e in range(nb) for i in range(pg)]
    in_specs += [page_spec(MLA_ROPE, PAGE_SIZE, e, i) for e in range(nb) for i in range(pg)]
    kern = functools.partial(_paged_kernel, ng=ng, pg=pg, t=t, nb=nb)
    return pl.pallas_call(
        kern,
        out_shape=jax.ShapeDtypeStruct((b, t, MLA_W), F32),
        grid_spec=pltpu.PrefetchScalarGridSpec(
            num_scalar_prefetch=1,
            grid=(b // nb, ng),
            in_specs=in_specs,
            out_specs=pl.BlockSpec((nb, t, MLA_W), lambda bi, g, pt: (bi, 0, 0)),
            scratch_shapes=[pltpu.VMEM((nb, rows, MLA_KV_LORA), BF16),
                            pltpu.VMEM((nb, rows, LANES), BF16),
                            pltpu.VMEM((nb, rows, 1), F32),
                            pltpu.VMEM((nb, rows, 1), F32),
                            pltpu.VMEM((nb, rows, MLA_KV_LORA), F32),
                            pltpu.VMEM((nb, pg * PAGE_SIZE, MLA_KV_LORA), BF16),
                            pltpu.VMEM((nb, MLA_ROPE, pg * PAGE_SIZE), BF16)],
        ),
        compiler_params=_cparams(("parallel", "arbitrary")),
        name="mla_paged",
    )(page_table, q, lat_new, kpe_new, wabs, wuv, *([lat_pool] * (nb * pg)), *([pe_pool] * (nb * pg)))


def _merge_kernel(h_ref, gate_ref, b0_ref, b1_ref, b2_ref, b3_ref, wbo_ref, wout_ref, o_ref):
    branches = (b0_ref, b1_ref, b2_ref, b3_ref)
    merged = None
    off = 0
    for i, br in enumerate(branches):
        wd = br.shape[1]
        proj = _dot(br[...].astype(BF16), wbo_ref[off:off + wd, :])
        term = jax.nn.sigmoid(gate_ref[:, i * D_MODEL:(i + 1) * D_MODEL].astype(F32)) * proj
        merged = term if merged is None else merged + term
        off += wd
    o_ref[...] = h_ref[...] + _dot(merged.astype(BF16), wout_ref[...])


def _merge(h, proj, branches, wbo, wout, *, tm):
    n = h.shape[0]
    row = lambda c: pl.BlockSpec((tm, c), lambda i: (i, 0))
    return pl.pallas_call(
        _merge_kernel,
        out_shape=jax.ShapeDtypeStruct((n, D_MODEL), F32),
        grid=(n // tm,),
        in_specs=[row(D_MODEL), row(GATE_W)] + [row(br.shape[1]) for br in branches]
        + [pl.BlockSpec(wbo.shape, lambda i: (0, 0)), pl.BlockSpec(wout.shape, lambda i: (0, 0))],
        out_specs=row(D_MODEL),
        compiler_params=_cparams(("parallel",)),
        name="branch_merge",
    )(h, proj, *branches, wbo, wout)


def _ffn_kernel(h_ref, g_ref, w1_ref, w2_ref, gf_ref, o_ref, *, tf, final_norm):
    h = h_ref[...]
    xn = _rms(h, g_ref[...]).astype(BF16)
    acc = None
    for j in range(w1_ref.shape[1] // tf):
        u = _dot(xn, w1_ref[:, j * tf:(j + 1) * tf])
        u = jnp.square(jnp.maximum(u, 0.0))
        part = _dot(u.astype(BF16), w2_ref[j * tf:(j + 1) * tf, :])
        acc = part if acc is None else acc + part
    out = h + acc
    if final_norm:
        out = _rms(out, gf_ref[...])
    o_ref[...] = out


def _ffn(h, g, w1, w2, gf, *, tm, tf, final_norm):
    n, d = h.shape
    f = w1.shape[1]
    kern = functools.partial(_ffn_kernel, tf=tf, final_norm=final_norm)
    return pl.pallas_call(
        kern,
        out_shape=jax.ShapeDtypeStruct((n, d), F32),
        grid=(n // tm,),
        in_specs=[
            pl.BlockSpec((tm, d), lambda i: (i, 0)),
            pl.BlockSpec((1, d), lambda i: (0, 0)),
            pl.BlockSpec((d, f), lambda i: (0, 0)),
            pl.BlockSpec((f, d), lambda i: (0, 0)),
            pl.BlockSpec((1, d), lambda i: (0, 0)),
        ],
        out_specs=pl.BlockSpec((tm, d), lambda i: (i, 0)),
        compiler_params=_cparams(("parallel",)),
        name="relu2_mlp",
    )(h, g, w1, w2, gf)


def _rot_partner(w, half):
    return jnp.concatenate([-w[..., half:], w[..., :half]], axis=-1)


def _pack_layer(w_in_l, conv_w_l, q_norm_l, w_uq_l, kv_norm_l, w_uk_l, w_uv_l, hgrn_norm_l,
                w_bo_l, w_out_l, g_mix_l, g_ffn_l, w_ff1_l, w_ff2_l):
    d = w_in_l.shape[0]
    offs = np.cumsum((0,) + IN_SIZES)
    col = lambda i: w_in_l[:, int(offs[i]):int(offs[i + 1])]
    rq, rk, rv, rg, cb, cc, cx, mcq, mckv, mkr, hq, hf, hi, hg, mg = [col(i) for i in range(15)]

    def halves(w):
        w4 = w.reshape(d, RET_HEADS, 2, RET_DK // 2)
        return w4[:, :, 0, :].reshape(d, -1), w4[:, :, 1, :].reshape(d, -1)

    rq1, rq2 = halves(rq)
    rk1, rk2 = halves(rk)
    mkr_rot = _rot_partner(mkr, MLA_ROPE // 2)
    pad = jnp.zeros((d, MLA_BLOCK_W - MLA_Q_LORA - MLA_KV_LORA - 2 * MLA_ROPE), w_in_l.dtype)
    packed = jnp.concatenate([mg, rv,
                              hq, hf, hi, hg, rq1, rq2, rk1, rk2, rg, cb, cc, cx,
                              mcq, mckv, mkr, mkr_rot, pad], axis=1).astype(BF16)

    qd = MLA_NOPE + MLA_ROPE
    wq = w_uq_l.reshape(MLA_Q_LORA, MLA_HEADS, qd)
    wq_nope = jnp.concatenate(
        [wq[:, :, :MLA_NOPE], jnp.zeros((MLA_Q_LORA, MLA_HEADS, MLA_HEAD_PAD - MLA_NOPE), wq.dtype)],
        axis=2).reshape(MLA_Q_LORA, MLA_HEADS * MLA_HEAD_PAD)
    wq_pe = wq[:, :, MLA_NOPE:]
    wq_rot = _rot_partner(wq_pe, MLA_ROPE // 2)
    e_q = np.zeros((MLA_HEADS * MLA_ROPE, MLA_HEADS * MLA_HEAD_PAD), np.float32)
    e_k = np.zeros((MLA_ROPE, MLA_HEADS * MLA_HEAD_PAD), np.float32)
    for h in range(MLA_HEADS):
        for i in range(MLA_ROPE):
            e_q[h * MLA_ROPE + i, h * MLA_HEAD_PAD + MLA_NOPE + i] = 1.0
            e_k[i, h * MLA_HEAD_PAD + MLA_NOPE + i] = 1.0
    wuk_pad = jnp.concatenate(
        [w_uk_l, jnp.zeros((MLA_KV_LORA, MLA_HEADS, MLA_HEAD_PAD - MLA_NOPE), w_uk_l.dtype)],
        axis=2).reshape(MLA_KV_LORA, MLA_HEADS * MLA_HEAD_PAD)
    wabs = jnp.zeros((MLA_HEADS, MLA_HEAD_PAD, MLA_KV_LORA + LANES), F32)
    wabs = wabs.at[:, :MLA_NOPE, :MLA_KV_LORA].set(jnp.transpose(w_uk_l, (1, 2, 0)))
    wabs = wabs.at[:, MLA_NOPE:MLA_NOPE + MLA_ROPE, MLA_KV_LORA:MLA_KV_LORA + MLA_ROPE].set(
        jnp.broadcast_to(jnp.eye(MLA_ROPE, dtype=F32), (MLA_HEADS, MLA_ROPE, MLA_ROPE)))
    wabs = wabs.reshape(MLA_HEADS * MLA_HEAD_PAD, MLA_KV_LORA + LANES)
    return {
        "w_in": packed,
        "g_mix": g_mix_l.reshape(1, d),
        "conv_w": conv_w_l,
        "q_norm": q_norm_l.reshape(1, -1),
        "kv_norm": kv_norm_l.reshape(1, -1),
        "wq_nope": wq_nope.astype(BF16),
        "wq_pe": wq_pe.reshape(MLA_Q_LORA, -1).astype(BF16),
        "wq_rot": wq_rot.reshape(MLA_Q_LORA, -1).astype(BF16),
        "e_q": jnp.asarray(e_q, BF16),
        "e_k": jnp.asarray(e_k, BF16),
        "wuk_pad": wuk_pad.astype(BF16),
        "wuv": w_uv_l.reshape(MLA_KV_LORA, MLA_W).astype(BF16),
        "wuv_t": w_uv_l.reshape(MLA_KV_LORA, MLA_W).T.astype(BF16),
        "wabs": wabs.astype(BF16),
        "hgrn_norm": hgrn_norm_l.reshape(1, -1),
        "w_bo": w_bo_l.astype(BF16),
        "w_out": w_out_l.astype(BF16),
        "g_ffn": g_ffn_l.reshape(1, d),
        "w_ff1": w_ff1_l.astype(BF16),
        "w_ff2": w_ff2_l.astype(BF16),
    }


def _rope_tables(pos):
    def cs(half):
        inv = ROPE_THETA ** (-jnp.arange(half, dtype=F32) / half)
        ang = pos.astype(F32)[:, None] * inv[None, :]
        return jnp.cos(ang), jnp.sin(ang)

    c_ret, s_ret = cs(RET_DK // 2)
    c_pe, s_pe = cs(MLA_ROPE // 2)
    two = lambda a: jnp.concatenate([a, a], axis=1)
    return {
        "cos_ret": jnp.tile(c_ret, (1, RET_HEADS)), "sin_ret": jnp.tile(s_ret, (1, RET_HEADS)),
        "cos_q": jnp.tile(two(c_pe), (1, MLA_HEADS)), "sin_q": jnp.tile(two(s_pe), (1, MLA_HEADS)),
        "cos_k": two(c_pe), "sin_k": two(s_pe),
    }


def _tiles(t):
    return {"conv": min(512, t), "hgrn": min(256, t), "prep": 512, "tq": min(512, t), "tk": min(512, t)}


def _trunk_layer(h, tabs, ret_state, conv_buf, hgrn_state, lb, lw, *, b, t, attend, final_gain):
    n = b * t
    tl = _tiles(t)
    odt = BF16 if t % 16 == 0 else F32
    proj_a, proj_b = _inproj(h, lw["g_mix"], lw["w_in"], tm=min(n, 2048), adt=odt)
    o_ret, ret_new = _retention(proj_a, proj_b, tabs["cos_ret"], tabs["sin_ret"], ret_state, b=b, t=t, odt=odt)
    o_conv, conv_new = _short_conv(proj_b, lw["conv_w"], conv_buf, b=b, t=t, rows=tl["conv"], odt=odt)
    o_h, hgrn_new = _hgrn(proj_b, lb, lw["hgrn_norm"], hgrn_state, b=b, t=t, rows=tl["hgrn"], odt=odt)
    o_mla, ckv, kpe = attend(proj_b, lw, tabs, tl, odt)
    h = _merge(h, proj_a, (o_ret, o_conv, o_mla, o_h), lw["w_bo"], lw["w_out"], tm=min(n, 512))
    gf = final_gain if final_gain is not None else lw["g_ffn"]
    h = _ffn(h, lw["g_ffn"], lw["w_ff1"], lw["w_ff2"], gf, tm=min(n, 512), tf=512,
             final_norm=final_gain is not None)
    return h, ret_new, conv_new, hgrn_new, ckv, kpe


def kernel(x_prompt, x_sample, cache_mla_latent, cache_mla_rope, page_table,
           state_retention, state_conv, state_hgrn,
           norm_mix, w_in, conv_w, mla_q_norm, mla_w_uq, mla_kv_norm, mla_w_uk, mla_w_uv,
           hgrn_lb_logits, hgrn_out_norm, w_branch_out, w_out, norm_ffn, w_ff1, w_ff2, norm_final):
    bp, tp, d = x_prompt.shape
    bs, ts, _ = x_sample.shape
    depth = w_in.shape[0]
    past_len = page_table.shape[1] * PAGE_SIZE
    tabs_p = _rope_tables(jnp.arange(tp, dtype=jnp.int32))
    tabs_s = _rope_tables(past_len + jnp.arange(ts, dtype=jnp.int32))
    lb_p = jax.nn.softmax(hgrn_lb_logits.astype(F32), axis=0)
    lower_bounds = jnp.cumsum(lb_p, axis=0) - lb_p[0:1]
    gfin = norm_final.reshape(1, d)
    rope_pool_t = jnp.swapaxes(cache_mla_rope, 2, 3)

    hp = x_prompt.reshape(bp * tp, d)
    hs = x_sample.reshape(bs * ts, d)
    zeros_ret = jnp.zeros((bp, RET_HEADS, RET_DK, RET_DV), F32)
    zeros_conv = jnp.zeros((bp, CONV_K - 1, CONV_WIDTH), F32)
    zeros_hgrn = jnp.zeros((bp, HGRN_HEADS, HGRN_DK, HGRN_DV), F32)
    outs = {k: [] for k in ("lat_p", "rope_p", "lat_s", "rope_s", "ret_p", "ret_s",
                            "conv_p", "conv_s", "hg_p", "hg_s")}
    for l in range(depth):
        lw = _pack_layer(w_in[l], conv_w[l], mla_q_norm[l], mla_w_uq[l], mla_kv_norm[l], mla_w_uk[l],
                         mla_w_uv[l], hgrn_out_norm[l], w_branch_out[l], w_out[l], norm_mix[l],
                         norm_ffn[l], w_ff1[l], w_ff2[l])
        lb = lower_bounds[l].reshape(1, HGRN_W)
        last = gfin if l == depth - 1 else None

        def attend_prompt(proj, lw, tabs, tl, odt):
            q, ckv, kpe, kc, v = _mla_prep(proj, lw, tabs, n=bp * tp, t=tp, tm=tl["prep"], with_kv=True)
            o = _flash(q, kc, v, b=bp, t=tp, tq=tl["tq"], tk=tl["tk"], odt=odt)
            return o, ckv, kpe

        def attend_sample(proj, lw, tabs, tl, odt, layer=l):
            q, ckv, kpe = _mla_prep(proj, lw, tabs, n=bs * ts, t=ts, tm=tl["prep"], with_kv=False)
            o = _paged_attention(page_table, q.reshape(bs, ts, -1), ckv.reshape(bs, ts, -1),
                                 kpe.reshape(bs, ts, -1), lw["wabs"], lw["wuv"],
                                 cache_mla_latent, rope_pool_t, layer=layer,
                                 pg=math.gcd(16, page_table.shape[1]), nb=math.gcd(2, bs))
            return o.reshape(bs * ts, MLA_W), ckv, kpe

        hp, r, c, g, ckv, kpe = _trunk_layer(hp, tabs_p, zeros_ret, zeros_conv, zeros_hgrn, lb, lw,
                                             b=bp, t=tp, attend=attend_prompt, final_gain=last)
        outs["ret_p"].append(r); outs["conv_p"].append(c); outs["hg_p"].append(g)
        outs["lat_p"].append(ckv.reshape(bp, tp, -1)); outs["rope_p"].append(kpe.reshape(bp, tp, -1))
        hs, r, c, g, ckv, kpe = _trunk_layer(hs, tabs_s, state_retention[l], state_conv[l], state_hgrn[l],
                                             lb, lw, b=bs, t=ts, attend=attend_sample, final_gain=last)
        outs["ret_s"].append(r); outs["conv_s"].append(c); outs["hg_s"].append(g)
        outs["lat_s"].append(ckv.reshape(bs, ts, -1)); outs["rope_s"].append(kpe.reshape(bs, ts, -1))

    st = lambda k: jnp.stack(outs[k])
    return (hp.reshape(bp, tp, d), hs.reshape(bs, ts, d),
            st("lat_p"), st("rope_p"), st("lat_s"), st("rope_s"),
            st("ret_p"), st("ret_s"), st("conv_p"), st("conv_s"), st("hg_p"), st("hg_s"))
```

```python
import functools
import math

import numpy as np
import jax
import jax.numpy as jnp
from jax import lax
from jax.experimental import pallas as pl
from jax.experimental.pallas import tpu as pltpu

F32 = jnp.float32
BF16 = jnp.bfloat16

D_MODEL = 1024
PAGE_SIZE = 128
RET_HEADS, RET_DK, RET_DV, RET_CHUNK = 4, 64, 128, 128
CONV_WIDTH, CONV_K = 512, 3
MLA_HEADS, MLA_Q_LORA, MLA_KV_LORA, MLA_NOPE, MLA_ROPE, MLA_V = 8, 384, 256, 64, 32, 64
MLA_SCALE = (MLA_NOPE + MLA_ROPE) ** -0.5
HGRN_HEADS, HGRN_DK, HGRN_DV = 4, 128, 128
D_FF = 4 * D_MODEL
ROPE_THETA = 10000.0
NORM_EPS = 1e-6
N_BRANCH = 4
NEG_BIG = -1e30
LOG2E = math.log2(math.e)

RET_W = RET_HEADS * RET_DV
MLA_W = MLA_HEADS * MLA_V
HGRN_W = HGRN_HEADS * HGRN_DV
IN_SIZES = (
    RET_HEADS * RET_DK, RET_HEADS * RET_DK, RET_W, RET_W,
    CONV_WIDTH, CONV_WIDTH, CONV_WIDTH,
    MLA_Q_LORA, MLA_KV_LORA, MLA_ROPE,
    HGRN_HEADS * HGRN_DK, HGRN_HEADS * HGRN_DK, HGRN_W, HGRN_W,
    N_BRANCH * D_MODEL,
)

LANES = 128
SUBLANES = 8
VMEM_LIMIT_BYTES = 56 * 1024 * 1024

IN_TILE = 768
GATE_W = N_BRANCH * D_MODEL
RV_OFF = GATE_W
PROJ_A_W = GATE_W + RET_W
HG_BLOCK_W = 4 * HGRN_W
RET_QKG_W = 1024
RET_QKG_OFF = HG_BLOCK_W
CONV_BLOCK_W = 3 * CONV_WIDTH
CONV_OFF = RET_QKG_OFF + RET_QKG_W
MLA_BLOCK_W = 768
MLA_OFF = CONV_OFF + CONV_BLOCK_W
PROJ_B_W = MLA_OFF + MLA_BLOCK_W
MLA_HEAD_PAD = 128
HGRN_SUB = 16
HGRN_CHUNK = 64
HGRN_SAFE_LOG_DECAY = 160.0
KEY_PAD = 128


def _cparams(sem):
    return pltpu.CompilerParams(dimension_semantics=sem, vmem_limit_bytes=VMEM_LIMIT_BYTES)


def _rms(x, g):
    return x * lax.rsqrt(jnp.mean(x * x, axis=-1, keepdims=True) + NORM_EPS) * g


def _silu(x):
    return x * jax.nn.sigmoid(x)


def _dot(a, b):
    return jnp.dot(a, b, preferred_element_type=F32)


def _dot_nt(a, b):
    return lax.dot_general(a, b, (((1,), (1,)), ((), ())), preferred_element_type=F32)


def _dot_tn(a, b):
    return lax.dot_general(a, b, (((0,), (0,)), ((), ())), preferred_element_type=F32)


def _inproj_kernel(x_ref, g_ref, w_ref, oa_ref, ob_ref, xn_ref, *, na):
    j = pl.program_id(1)

    @pl.when(j == 0)
    def _():
        xn_ref[...] = _rms(x_ref[...], g_ref[...]).astype(BF16)

    @pl.when(j < na)
    def _():
        oa_ref[...] = _dot(xn_ref[...], w_ref[...]).astype(oa_ref.dtype)

    @pl.when(j >= na)
    def _():
        ob_ref[...] = _dot(xn_ref[...], w_ref[...])


def _inproj(x, g, w, *, tm, adt):
    n, d = x.shape
    tn = IN_TILE
    na = PROJ_A_W // tn
    nb = PROJ_B_W // tn
    return pl.pallas_call(
        functools.partial(_inproj_kernel, na=na),
        out_shape=(jax.ShapeDtypeStruct((n, PROJ_A_W), adt), jax.ShapeDtypeStruct((n, PROJ_B_W), F32)),
        grid=(n // tm, na + nb),
        in_specs=[
            pl.BlockSpec((tm, d), lambda i, j: (i, 0)),
            pl.BlockSpec((1, d), lambda i, j: (0, 0)),
            pl.BlockSpec((d, tn), lambda i, j: (0, j)),
        ],
        out_specs=(pl.BlockSpec((tm, tn), lambda i, j: (i, jnp.minimum(j, na - 1))),
                   pl.BlockSpec((tm, tn), lambda i, j: (i, jnp.maximum(j - na, 0)))),
        scratch_shapes=[pltpu.VMEM((tm, d), BF16)],
        compiler_params=_cparams(("parallel", "arbitrary")),
        name="in_proj",
    )(x, g, w)


def _ret_kernel(x_ref, v_ref, cos_ref, sin_ref, decay_ref, qdec_ref, kdec_ref, s0_ref,
                o_ref, sout_ref, s_ref, *, nt, cdec, c):
    t = pl.program_id(1)
    hw = RET_HEADS * RET_DK // 2
    per = RET_DK // 2

    @pl.when(t == 0)
    def _():
        s_ref[...] = jnp.zeros(s_ref.shape, F32)
        for h in range(RET_HEADS):
            s_ref[h, h * per:(h + 1) * per, :] = s0_ref[0, h, 0:per, :]
            s_ref[h, hw + h * per:hw + (h + 1) * per, :] = s0_ref[0, h, per:2 * per, :]

    lane = lax.broadcasted_iota(jnp.int32, (1, 2 * hw), 1)
    head_of_lane = (lane % hw) // per
    for sub in range(x_ref.shape[0] // c):
        r0 = sub * c
        x = x_ref[r0:r0 + c, :]
        cos = cos_ref[r0:r0 + c, :]
        sin = sin_ref[r0:r0 + c, :]
        q1, q2 = x[:, 0:hw], x[:, hw:2 * hw]
        k1, k2 = x[:, 2 * hw:3 * hw], x[:, 3 * hw:4 * hw]
        qr = jnp.concatenate([q1 * cos - q2 * sin, q1 * sin + q2 * cos], axis=1)
        kr = jnp.concatenate([k1 * cos - k2 * sin, k1 * sin + k2 * cos], axis=1) * (RET_DK ** -0.5)
        v = v_ref[r0:r0 + c, :]
        g = x[:, 4 * hw:4 * hw + RET_W]
        if c < KEY_PAD:
            kr = jnp.concatenate([kr, jnp.zeros((KEY_PAD - c, kr.shape[1]), F32)], axis=0)
            v = jnp.concatenate([v, jnp.zeros((KEY_PAD - c, v.shape[1]), v.dtype)], axis=0)
        kt = kr.T
        ktb = kt.astype(BF16)
        qms = [jnp.where(head_of_lane == h, qr, 0.0).astype(BF16) for h in range(RET_HEADS)]
        atts = [(_dot(qms[h], ktb) * decay_ref[h]).astype(BF16) for h in range(RET_HEADS)]
        for h in range(RET_HEADS):
            qm = qms[h]
            vh = v[:, h * RET_DV:(h + 1) * RET_DV].astype(BF16)
            sh = s_ref[h]
            o = _dot(atts[h], vh) + _dot(qm, sh.astype(BF16)) * qdec_ref[h]
            s_ref[h] = sh * cdec[h] + _dot((kt * kdec_ref[h]).astype(BF16), vh)
            o = o * lax.rsqrt(jnp.mean(o * o, axis=-1, keepdims=True) + NORM_EPS)
            gh = g[:, h * RET_DV:(h + 1) * RET_DV]
            o_ref[r0:r0 + c, h * RET_DV:(h + 1) * RET_DV] = (o * _silu(gh)).astype(o_ref.dtype)

    @pl.when(t == nt - 1)
    def _():
        for h in range(RET_HEADS):
            sout_ref[0, h, 0:per, :] = s_ref[h, h * per:(h + 1) * per, :]
            sout_ref[0, h, per:2 * per, :] = s_ref[h, hw + h * per:hw + (h + 1) * per, :]


def _retention(proj_a, proj_b, cos, sin, state, *, b, t, odt):
    c = math.gcd(t, RET_CHUNK)
    rows = c * math.gcd(t // c, 2)
    nt = t // rows
    ck = max(c, KEY_PAD)
    hw = RET_HEADS * RET_DK // 2
    hs = np.arange(RET_HEADS, dtype=np.float64)
    log_gamma = np.log1p(-np.exp2(-5.0 - hs))
    j = np.arange(c, dtype=np.float64)
    rel = j[:, None] - j[None, :]
    decay = np.where(rel >= 0, np.exp(log_gamma[:, None, None] * np.where(rel >= 0, rel, 0.0)), 0.0)
    decay_p = np.zeros((RET_HEADS, c, ck))
    decay_p[:, :, :c] = decay
    qdec = np.repeat(np.exp(log_gamma[:, None] * (j + 1.0))[:, :, None], RET_DV, axis=2)
    kdec = np.zeros((RET_HEADS, 1, ck))
    kdec[:, 0, :c] = np.exp(log_gamma[:, None] * (c - 1.0 - j))
    cdec = tuple(float(np.float32(np.exp(lg * c))) for lg in log_gamma)
    kern = functools.partial(_ret_kernel, nt=nt, cdec=cdec, c=c)
    return pl.pallas_call(
        kern,
        out_shape=(jax.ShapeDtypeStruct((b * t, RET_W), odt),
                   jax.ShapeDtypeStruct((b, RET_HEADS, RET_DK, RET_DV), F32)),
        grid=(b, nt),
        in_specs=[
            pl.BlockSpec((rows, RET_QKG_W), lambda i, k: (i * nt + k, RET_QKG_OFF // RET_QKG_W)),
            pl.BlockSpec((rows, RET_W), lambda i, k: (i * nt + k, RV_OFF // RET_W)),
            pl.BlockSpec((rows, hw), lambda i, k: (k, 0)),
            pl.BlockSpec((rows, hw), lambda i, k: (k, 0)),
            pl.BlockSpec((RET_HEADS, c, ck), lambda i, k: (0, 0, 0)),
            pl.BlockSpec((RET_HEADS, c, RET_DV), lambda i, k: (0, 0, 0)),
            pl.BlockSpec((RET_HEADS, 1, ck), lambda i, k: (0, 0, 0)),
            pl.BlockSpec((1, RET_HEADS, RET_DK, RET_DV), lambda i, k: (i, 0, 0, 0)),
        ],
        out_specs=(pl.BlockSpec((rows, RET_W), lambda i, k: (i * nt + k, 0)),
                   pl.BlockSpec((1, RET_HEADS, RET_DK, RET_DV), lambda i, k: (i, 0, 0, 0))),
        scratch_shapes=[pltpu.VMEM((RET_HEADS, RET_HEADS * RET_DK, RET_DV), F32)],
        compiler_params=_cparams(("parallel", "arbitrary")),
        name="retention",
    )(proj_b, proj_a, cos, sin, jnp.asarray(decay_p, F32), jnp.asarray(qdec, F32), jnp.asarray(kdec, F32), state)


def _conv_kernel(x_ref, w_ref, buf_ref, o_ref, bout_ref, carry_ref, *, nt):
    t = pl.program_id(1)
    w = CONV_WIDTH

    @pl.when(t == 0)
    def _():
        carry_ref[...] = jnp.zeros(carry_ref.shape, F32)
        carry_ref[SUBLANES - 2:SUBLANES, :] = buf_ref[0]

    x = x_ref[...]
    rows = x.shape[0]
    gate_b, gate_c, xin = x[:, 0:w], x[:, w:2 * w], x[:, 2 * w:3 * w]
    u = gate_c * xin
    row = lax.broadcasted_iota(jnp.int32, (rows, 1), 0)
    p1 = carry_ref[SUBLANES - 1:SUBLANES, :]
    p2 = carry_ref[SUBLANES - 2:SUBLANES - 1, :]
    u1 = jnp.where(row == 0, p1, pltpu.roll(u, 1, 0))
    u2 = jnp.where(row == 0, p2, jnp.where(row == 1, p1, pltpu.roll(u, 2, 0)))
    y = u2 * w_ref[0:1, :] + u1 * w_ref[1:2, :] + u * w_ref[2:3, :]
    o_ref[...] = (gate_b * y).astype(o_ref.dtype)
    carry_ref[...] = u[rows - SUBLANES:rows, :]

    @pl.when(t == nt - 1)
    def _():
        bout_ref[0] = u[rows - 2:rows, :]


def _short_conv(proj, w, buf, *, b, t, rows, odt):
    nt = t // rows
    kern = functools.partial(_conv_kernel, nt=nt)
    return pl.pallas_call(
        kern,
        out_shape=(jax.ShapeDtypeStruct((b * t, CONV_WIDTH), odt),
                   jax.ShapeDtypeStruct((b, CONV_K - 1, CONV_WIDTH), F32)),
        grid=(b, nt),
        in_specs=[
            pl.BlockSpec((rows, CONV_BLOCK_W), lambda i, k: (i * nt + k, CONV_OFF // CONV_BLOCK_W)),
            pl.BlockSpec((CONV_K, CONV_WIDTH), lambda i, k: (0, 0)),
            pl.BlockSpec((1, CONV_K - 1, CONV_WIDTH), lambda i, k: (i, 0, 0)),
        ],
        out_specs=(pl.BlockSpec((rows, CONV_WIDTH), lambda i, k: (i * nt + k, 0)),
                   pl.BlockSpec((1, CONV_K - 1, CONV_WIDTH), lambda i, k: (i, 0, 0))),
        scratch_shapes=[pltpu.VMEM((SUBLANES, CONV_WIDTH), F32)],
        compiler_params=_cparams(("parallel", "arbitrary")),
        name="short_conv",
    )(proj, w, buf)


def _split3(x):
    hi = x.astype(BF16)
    r = x - hi.astype(F32)
    mid = r.astype(BF16)
    lo = (r - mid.astype(F32)).astype(BF16)
    return hi, mid, lo


def _hgrn_intra_exact(gcum, qh, kk, hv, g_ref, k_ref, oi_ref):
    ell = gcum.shape[0]
    dk = HGRN_DK
    g_ref[...] = gcum
    k_ref[...] = kk
    nsub = ell // HGRN_SUB
    lane = lax.broadcasted_iota(jnp.int32, (1, KEY_PAD), 1)
    key_row = lax.broadcasted_iota(jnp.int32, (ell, 1), 0)
    sub_row = lax.broadcasted_iota(jnp.int32, (HGRN_SUB, 1), 0)
    a_blocks = [[None] * nsub for _ in range(HGRN_HEADS)]
    for j in range(nsub):
        r0 = j * HGRN_SUB
        gq = gcum[r0:r0 + HGRN_SUB, :]
        qq = qh[r0:r0 + HGRN_SUB, :]

        def diag_body(s, carry, r0=r0, gq=gq, qq=qq):
            gs = g_ref[pl.ds(r0 + s, 1), :]
            ks = k_ref[pl.ds(r0 + s, 1), :]
            valid = sub_row >= s
            p = jnp.where(valid, jnp.exp(jnp.minimum(gq - gs, 0.0)) * qq * ks, 0.0)
            out = []
            for h in range(HGRN_HEADS):
                a = jnp.sum(p[:, h * dk:(h + 1) * dk], axis=-1, keepdims=True)
                out.append(carry[h] + jnp.where(lane == r0 + s, a, 0.0))
            return tuple(out)

        init = tuple(jnp.zeros((HGRN_SUB, KEY_PAD), F32) for _ in range(HGRN_HEADS))
        diag = lax.fori_loop(0, HGRN_SUB, diag_body, init)
        if j > 0:
            gb = gcum[r0 - 1:r0, :]
            q_t = (qq * jnp.exp(gq - gb)).astype(BF16)
            k_t = jnp.where(key_row < r0, kk * jnp.exp(jnp.minimum(gb - gcum, 0.0)), 0.0)
            if ell < KEY_PAD:
                k_t = jnp.concatenate([k_t, jnp.zeros((KEY_PAD - ell, k_t.shape[1]), F32)], axis=0)
            k_t = k_t.astype(BF16)
        for h in range(HGRN_HEADS):
            blk = diag[h]
            if j > 0:
                blk = blk + _dot_nt(q_t[:, h * dk:(h + 1) * dk], k_t[:, h * dk:(h + 1) * dk])
            a_blocks[h][j] = blk
    for h in range(HGRN_HEADS):
        a_h = a_blocks[h][0] if nsub == 1 else jnp.concatenate(a_blocks[h], axis=0)
        vh = hv[:, h * HGRN_DV:(h + 1) * HGRN_DV]
        if ell < KEY_PAD:
            vh = jnp.concatenate([vh, jnp.zeros((KEY_PAD - ell, HGRN_DV), F32)], axis=0)
        oi_ref[h] = _dot(a_h.astype(BF16), vh.astype(BF16))


def _hgrn_intra_fast(parts, oi_ref):
    dk = HGRN_DK
    ell = parts[0][0].shape[0]
    causal = lax.broadcasted_iota(jnp.int32, (ell, ell), 0) >= lax.broadcasted_iota(jnp.int32, (ell, ell), 1)
    scores = []
    for gcum, g_end, qh, kk, _, _, _ in parts:
        gm = 0.5 * g_end
        q_t = (qh * jnp.exp(gcum - gm)).astype(BF16)
        k_t = (kk * jnp.exp(gm - gcum)).astype(BF16)
        scores.append([_dot_nt(q_t[:, h * dk:(h + 1) * dk], k_t[:, h * dk:(h + 1) * dk])
                       for h in range(HGRN_HEADS)])
    masked = [[jnp.where(causal, a, 0.0).astype(BF16) for a in row] for row in scores]
    for c, prt in enumerate(parts):
        hv = prt[4]
        for h in range(HGRN_HEADS):
            oi_ref[c, h] = _dot(masked[c][h], hv[:, h * HGRN_DV:(h + 1) * HGRN_DV].astype(BF16))


def _hgrn_kernel(x_ref, lb_ref, gn_ref, s0_ref, o_ref, sout_ref,
                 st_ref, g_ref, k_ref, oi_ref, *, nt, chunk):
    t = pl.program_id(1)
    w = HGRN_W
    dk = HGRN_DK

    @pl.when(t == 0)
    def _():
        for h in range(HGRN_HEADS):
            st_ref[h] = s0_ref[0, h].T

    rows = x_ref.shape[0]
    lb = lb_ref[...]
    gn = gn_ref[...]
    ri = lax.broadcasted_iota(jnp.int32, (chunk, chunk), 0)
    ci = lax.broadcasted_iota(jnp.int32, (chunk, chunk), 1)
    tri = jnp.where(ri >= ci, 1.0, 0.0).astype(BF16)
    parts = []
    for c in range(max(rows // chunk, 1)):
        if rows >= chunk:
            x = x_ref[c * chunk:(c + 1) * chunk, :]
            live = chunk
        else:
            x = jnp.concatenate([x_ref[...], jnp.zeros((chunk - rows, x_ref.shape[1]), F32)], axis=0)
            live = rows
        hq, hf, hv, hg = x[:, 0:w], x[:, w:2 * w], x[:, 2 * w:3 * w], x[:, 3 * w:4 * w]
        f_gate = lb + (1.0 - lb) * jax.nn.sigmoid(hf)
        log_f = jnp.log(f_gate)
        kk = 1.0 - f_gate
        if live < chunk:
            ok = lax.broadcasted_iota(jnp.int32, (chunk, 1), 0) < live
            log_f = jnp.where(ok, log_f, 0.0)
            kk = jnp.where(ok, kk, 0.0)
        qh = _silu(hq)
        p_hi, p_mid, p_lo = _split3(log_f)
        gcum = _dot(tri, p_hi) + _dot(tri, p_mid) + _dot(tri, p_lo)
        parts.append((gcum, gcum[chunk - 1:chunk, :], qh, kk, hv, hg, live))

    g_min = parts[0][1]
    for prt in parts[1:]:
        g_min = jnp.minimum(g_min, prt[1])
    safe = jnp.min(g_min) > -HGRN_SAFE_LOG_DECAY

    @pl.when(safe)
    def _():
        _hgrn_intra_fast(parts, oi_ref)

    @pl.when(jnp.logical_not(safe))
    def _():
        for c, (gcum, g_end, qh, kk, hv, _, _) in enumerate(parts):
            _hgrn_intra_exact(gcum, qh, kk, hv, g_ref, k_ref, oi_ref.at[c])

    for c, (gcum, g_end, qh, kk, hv, hg, live) in enumerate(parts):
        q_in = (qh * jnp.exp(gcum)).astype(BF16)
        k_out = (kk * jnp.exp(g_end - gcum)).astype(BF16)
        e_end = jnp.exp(g_end)
        for h in range(HGRN_HEADS):
            sl = slice(h * dk, (h + 1) * dk)
            st = st_ref[h]
            o = oi_ref[c, h] + _dot_nt(q_in[:, sl], st.astype(BF16))
            st_ref[h] = st * e_end[:, sl] + _dot_tn(hv[:, h * HGRN_DV:(h + 1) * HGRN_DV].astype(BF16), k_out[:, sl])
            o = _rms(o, gn) * _silu(hg[:, h * HGRN_DV:(h + 1) * HGRN_DV])
            o_ref[c * chunk:c * chunk + live, h * HGRN_DV:(h + 1) * HGRN_DV] = o[0:live, :].astype(o_ref.dtype)

    @pl.when(t == nt - 1)
    def _():
        for h in range(HGRN_HEADS):
            sout_ref[0, h] = st_ref[h].T


def _hgrn(proj, lb, gnorm, state, *, b, t, rows, odt):
    nt = t // rows
    chunk = max(min(rows, HGRN_CHUNK), HGRN_SUB)
    nchunk = max(rows // chunk, 1)
    kern = functools.partial(_hgrn_kernel, nt=nt, chunk=chunk)
    return pl.pallas_call(
        kern,
        out_shape=(jax.ShapeDtypeStruct((b * t, HGRN_W), odt),
                   jax.ShapeDtypeStruct((b, HGRN_HEADS, HGRN_DK, HGRN_DV), F32)),
        grid=(b, nt),
        in_specs=[
            pl.BlockSpec((rows, HG_BLOCK_W), lambda i, k: (i * nt + k, 0)),
            pl.BlockSpec((1, HGRN_W), lambda i, k: (0, 0)),
            pl.BlockSpec((1, HGRN_DV), lambda i, k: (0, 0)),
            pl.BlockSpec((1, HGRN_HEADS, HGRN_DK, HGRN_DV), lambda i, k: (i, 0, 0, 0)),
        ],
        out_specs=(pl.BlockSpec((rows, HGRN_W), lambda i, k: (i * nt + k, 0)),
                   pl.BlockSpec((1, HGRN_HEADS, HGRN_DK, HGRN_DV), lambda i, k: (i, 0, 0, 0))),
        scratch_shapes=[pltpu.VMEM((HGRN_HEADS, HGRN_DV, HGRN_DK), F32),
                        pltpu.VMEM((chunk, HGRN_W), F32),
                        pltpu.VMEM((chunk, HGRN_W), F32),
                        pltpu.VMEM((nchunk, HGRN_HEADS, chunk, HGRN_DV), F32)],
        compiler_params=_cparams(("parallel", "arbitrary")),
        name="hgrn2",
    )(proj, lb, gnorm, state)


def _mla_prep_kernel(x_ref, qn_ref, kvn_ref, wqn_ref, wqp_ref, wqr_ref, eq_ref,
                     cq_ref, sq_ref, ck_ref, sk_ref, *rest, with_kv):
    if with_kv:
        wuk_ref, ek_ref, wuv_ref, q_ref, lat_ref, kpe_ref, kc_ref, v_ref = rest
    else:
        q_ref, lat_ref, kpe_ref = rest
    x = x_ref[...]
    mcq = x[:, 0:MLA_Q_LORA]
    mckv = x[:, MLA_Q_LORA:MLA_Q_LORA + MLA_KV_LORA]
    o = MLA_Q_LORA + MLA_KV_LORA
    mkr = x[:, o:o + MLA_ROPE]
    mkr_rot = x[:, o + MLA_ROPE:o + 2 * MLA_ROPE]
    xq = _rms(mcq, qn_ref[...]).astype(BF16)
    q_pe = _dot(xq, wqp_ref[...]) * cq_ref[...] + _dot(xq, wqr_ref[...]) * sq_ref[...]
    q_cat = _dot(xq, wqn_ref[...]) + _dot(q_pe.astype(BF16), eq_ref[...])
    q_ref[...] = (q_cat * (MLA_SCALE * LOG2E)).astype(q_ref.dtype)
    ckv = _rms(mckv, kvn_ref[...])
    lat_ref[...] = ckv
    kpe = mkr * ck_ref[...] + mkr_rot * sk_ref[...]
    kpe_ref[...] = kpe
    if with_kv:
        cb = ckv.astype(BF16)
        kc_ref[...] = (_dot(cb, wuk_ref[...]) + _dot(kpe.astype(BF16), ek_ref[...])).astype(BF16)
        v_ref[0] = _dot_nt(wuv_ref[...], cb).astype(BF16)


def _mla_prep(proj, lw, tabs, *, n, t, tm, with_kv):
    rows = min(tm, t)
    nt = t // rows
    tables = [tabs["cos_q"], tabs["sin_q"], tabs["cos_k"], tabs["sin_k"]]
    if t < tm and not with_kv:
        reps = math.gcd(tm // t, n // t)
        rows = t * reps
        tables = [jnp.tile(a, (reps, 1)) for a in tables]
    hp = MLA_HEADS * MLA_HEAD_PAD
    qp_w = MLA_HEADS * MLA_ROPE
    full = lambda r, c: pl.BlockSpec((r, c), lambda i: (0, 0))
    tab = lambda c: pl.BlockSpec((rows, c), lambda i: (i % nt, 0))
    in_specs = [
        pl.BlockSpec((rows, MLA_BLOCK_W), lambda i: (i, MLA_OFF // MLA_BLOCK_W)),
        full(1, MLA_Q_LORA), full(1, MLA_KV_LORA),
        full(MLA_Q_LORA, hp), full(MLA_Q_LORA, qp_w), full(MLA_Q_LORA, qp_w), full(qp_w, hp),
        tab(qp_w), tab(qp_w), tab(MLA_ROPE), tab(MLA_ROPE),
    ]
    args = [proj, lw["q_norm"], lw["kv_norm"], lw["wq_nope"], lw["wq_pe"], lw["wq_rot"], lw["e_q"]] + tables
    q_dtype = BF16 if with_kv else F32
    out_shape = [jax.ShapeDtypeStruct((n, hp), q_dtype),
                 jax.ShapeDtypeStruct((n, MLA_KV_LORA), F32),
                 jax.ShapeDtypeStruct((n, MLA_ROPE), F32)]
    out_specs = [pl.BlockSpec((rows, hp), lambda i: (i, 0)),
                 pl.BlockSpec((rows, MLA_KV_LORA), lambda i: (i, 0)),
                 pl.BlockSpec((rows, MLA_ROPE), lambda i: (i, 0))]
    if with_kv:
        in_specs += [full(MLA_KV_LORA, hp), full(MLA_ROPE, hp), full(MLA_W, MLA_KV_LORA)]
        args += [lw["wuk_pad"], lw["e_k"], lw["wuv_t"]]
        out_shape += [jax.ShapeDtypeStruct((n, hp), BF16), jax.ShapeDtypeStruct((n // t, MLA_W, t), BF16)]
        out_specs += [pl.BlockSpec((rows, hp), lambda i: (i, 0)),
                      pl.BlockSpec((1, MLA_W, rows), lambda i: (i // nt, 0, i % nt))]
    return pl.pallas_call(
        functools.partial(_mla_prep_kernel, with_kv=with_kv),
        out_shape=tuple(out_shape),
        grid=(n // rows,),
        in_specs=in_specs,
        out_specs=tuple(out_specs),
        compiler_params=_cparams(("parallel",)),
        name="mla_prep",
    )(*args)


def _flash_kernel(qi_ref, kj_ref, q_ref, k_ref, vt_ref, o_ref, m_ref, l_ref, acc_ref, *, tq, tk):
    step_id = pl.program_id(1)
    qi = qi_ref[step_id]
    kj = kj_ref[step_id]
    hp = MLA_HEAD_PAD

    @pl.when(kj == 0)
    def _():
        m_ref[...] = jnp.full(m_ref.shape, NEG_BIG, F32)
        l_ref[...] = jnp.zeros(l_ref.shape, F32)
        acc_ref[...] = jnp.zeros(acc_ref.shape, F32)

    def scores(h):
        return _dot_nt(k_ref[:, h * hp:(h + 1) * hp], q_ref[:, h * hp:(h + 1) * hp])

    def step(mask):
        ahead = 2
        pending = [scores(h) for h in range(ahead)]
        for h in range(MLA_HEADS):
            s = pending.pop(0)
            if h + ahead < MLA_HEADS:
                pending.append(scores(h + ahead))
            if mask is not None:
                s = jnp.where(mask, s, NEG_BIG)
            m_old = m_ref[h]
            m_new = jnp.maximum(m_old, jnp.max(s, axis=0, keepdims=True))
            alpha = jnp.exp2(m_old - m_new)
            p = jnp.exp2(s - m_new)
            l_ref[h] = l_ref[h] * alpha + jnp.sum(p, axis=0, keepdims=True)
            vt = vt_ref[0, h * MLA_V:(h + 1) * MLA_V, :]
            acc_ref[h] = acc_ref[h] * alpha + _dot(vt, p.astype(BF16))
            m_ref[h] = m_new

    @pl.when(kj < qi)
    def _():
        step(None)

    @pl.when(kj == qi)
    def _():
        step(lax.broadcasted_iota(jnp.int32, (tk, tq), 0) <= lax.broadcasted_iota(jnp.int32, (tk, tq), 1))
        for h in range(MLA_HEADS):
            acc_ref[h] = acc_ref[h] / l_ref[h]
        o_ref[...] = acc_ref[...].reshape(MLA_W, tq).T.astype(o_ref.dtype)


def _flash(q, k, vt, *, b, t, tq, tk, odt):
    nq, nk = t // tq, t // tk
    hp = MLA_HEADS * MLA_HEAD_PAD
    kern = functools.partial(_flash_kernel, tq=tq, tk=tk)
    pairs = [(a, c) for a in range(nq) for c in range(a + 1)]
    qi_tab = jnp.asarray([p[0] for p in pairs], jnp.int32)
    kj_tab = jnp.asarray([p[1] for p in pairs], jnp.int32)
    return pl.pallas_call(
        kern,
        out_shape=jax.ShapeDtypeStruct((b * t, MLA_W), odt),
        grid_spec=pltpu.PrefetchScalarGridSpec(
            num_scalar_prefetch=2,
            grid=(b, len(pairs)),
            in_specs=[
                pl.BlockSpec((tq, hp), lambda i, s, qt, kt: (i * nq + qt[s], 0)),
                pl.BlockSpec((tk, hp), lambda i, s, qt, kt: (i * nk + kt[s], 0)),
                pl.BlockSpec((1, MLA_W, tk), lambda i, s, qt, kt: (i, 0, kt[s])),
            ],
            out_specs=pl.BlockSpec((tq, MLA_W), lambda i, s, qt, kt: (i * nq + qt[s], 0)),
            scratch_shapes=[pltpu.VMEM((MLA_HEADS, 1, tq), F32),
                            pltpu.VMEM((MLA_HEADS, 1, tq), F32),
                            pltpu.VMEM((MLA_HEADS, MLA_V, tq), F32)],
        ),
        compiler_params=_cparams(("parallel", "arbitrary")),
        name="mla_flash",
    )(qi_tab, kj_tab, q, k, vt)


def _paged_kernel(pt_ref, q_ref, lat_new_ref, kpe_new_ref, wabs_ref, wuv_ref, *rest,
                  ng, pg, t, nb):
    npg = nb * pg
    lat_refs = rest[0:npg]
    pe_refs = rest[npg:2 * npg]
    o_ref = rest[2 * npg]
    ql_ref, qp_ref, m_ref, l_ref, acc_ref, c_ref, rt_ref = rest[2 * npg + 1:]
    g = pl.program_id(1)
    hp = MLA_HEAD_PAD
    rows = MLA_HEADS * t

    @pl.when(g == 0)
    def _():
        lane_head = lax.broadcasted_iota(jnp.int32, (1, MLA_HEADS * hp), 1) // hp
        for e in range(nb):
            q = q_ref[e]
            q_exp = jnp.concatenate(
                [jnp.where(lane_head == h, q, 0.0) for h in range(MLA_HEADS)], axis=0).astype(BF16)
            q_abs = _dot(q_exp, wabs_ref[...])
            ql_ref[e] = q_abs[:, 0:MLA_KV_LORA].astype(BF16)
            qp_ref[e] = q_abs[:, MLA_KV_LORA:MLA_KV_LORA + LANES].astype(BF16)
        m_ref[...] = jnp.full(m_ref.shape, NEG_BIG, F32)
        l_ref[...] = jnp.zeros(l_ref.shape, F32)
        acc_ref[...] = jnp.zeros(acc_ref.shape, F32)

    def update(e, s, c):
        m_old = m_ref[e]
        m_new = jnp.maximum(m_old, jnp.max(s, axis=-1, keepdims=True))
        alpha = jnp.exp2(m_old - m_new)
        p = jnp.exp2(s - m_new)
        l_ref[e] = l_ref[e] * alpha + jnp.sum(p, axis=-1, keepdims=True)
        acc_ref[e] = acc_ref[e] * alpha + _dot(p.astype(BF16), c)
        m_ref[e] = m_new

    for e in range(nb):
        for i in range(pg):
            c_ref[e, i * PAGE_SIZE:(i + 1) * PAGE_SIZE, :] = lat_refs[e * pg + i][0, 0].astype(BF16)
            rt_ref[e, :, i * PAGE_SIZE:(i + 1) * PAGE_SIZE] = pe_refs[e * pg + i][0, 0].astype(BF16)
    s_all = [_dot_nt(ql_ref[e], c_ref[e]) + _dot(qp_ref[e, :, 0:MLA_ROPE], rt_ref[e]) for e in range(nb)]
    for e in range(nb):
        update(e, s_all[e], c_ref[e])

    @pl.when(g == ng - 1)
    def _():
        qtok = lax.broadcasted_iota(jnp.int32, (rows, KEY_PAD), 0) % t
        kidx = lax.broadcasted_iota(jnp.int32, (rows, KEY_PAD), 1)
        lane_head = lax.broadcasted_iota(jnp.int32, (1, MLA_W), 1) // MLA_V
        for e in range(nb):
            cn = jnp.concatenate([lat_new_ref[e], jnp.zeros((KEY_PAD - t, MLA_KV_LORA), F32)], axis=0).astype(BF16)
            rn = jnp.concatenate([kpe_new_ref[e], jnp.zeros((KEY_PAD - t, MLA_ROPE), F32)], axis=0).astype(BF16)
            s = _dot_nt(ql_ref[e], cn) + _dot_nt(qp_ref[e, :, 0:MLA_ROPE], rn)
            update(e, jnp.where(kidx <= qtok, s, NEG_BIG), cn)
            o_lat = acc_ref[e] / l_ref[e]
            res = _dot(o_lat.astype(BF16), wuv_ref[...])
            out = jnp.zeros((t, MLA_W), F32)
            for h in range(MLA_HEADS):
                out = out + jnp.where(lane_head == h, res[h * t:(h + 1) * t, :], 0.0)
            o_ref[e] = out


def _paged_attention(page_table, q, lat_new, kpe_new, wabs, wuv, lat_pool, pe_pool, *, layer, pg, nb):
    b, t, hp = q.shape
    n_pages = page_table.shape[1]
    ng = n_pages // pg
    rows = MLA_HEADS * t

    def page_spec(r, c, e, i):
        return pl.BlockSpec((1, 1, r, c), lambda bi, g, pt: (layer, pt[bi * nb + e, g * pg + i], 0, 0))

    in_specs = [
        pl.BlockSpec((nb, t, hp), lambda bi, g, pt: (bi, 0, 0)),
        pl.BlockSpec((nb, t, MLA_KV_LORA), lambda bi, g, pt: (bi, 0, 0)),
        pl.BlockSpec((nb, t, MLA_ROPE), lambda bi, g, pt: (bi, 0, 0)),
        pl.BlockSpec(wabs.shape, lambda bi, g, pt: (0, 0)),
        pl.BlockSpec(wuv.shape, lambda bi, g, pt: (0, 0)),
    ]
    in_specs += [page_spec(PAGE_SIZE, MLA_KV_LORA, e, i) for e in range(nb) for i in range(pg)]
    in_specs += [page_spec(MLA_ROPE, PAGE_SIZE, e, i) for e in range(nb) for i in range(pg)]
    kern = functools.partial(_paged_kernel, ng=ng, pg=pg, t=t, nb=nb)
    return pl.pallas_call(
        kern,
        out_shape=jax.ShapeDtypeStruct((b, t, MLA_W), F32),
        grid_spec=pltpu.PrefetchScalarGridSpec(
            num_scalar_prefetch=1,
            grid=(b // nb, ng),
            in_specs=in_specs,
            out_specs=pl.BlockSpec((nb, t, MLA_W), lambda bi, g, pt: (bi, 0, 0)),
            scratch_shapes=[pltpu.VMEM((nb, rows, MLA_KV_LORA), BF16),
                            pltpu.VMEM((nb, rows, LANES), BF16),
                            pltpu.VMEM((nb, rows, 1), F32),
                            pltpu.VMEM((nb, rows, 1), F32),
                            pltpu.VMEM((nb, rows, MLA_KV_LORA), F32),
                            pltpu.VMEM((nb, pg * PAGE_SIZE, MLA_KV_LORA), BF16),
                            pltpu.VMEM((nb, MLA_ROPE, pg * PAGE_SIZE), BF16)],
        ),
        compiler_params=_cparams(("parallel", "arbitrary")),
        name="mla_paged",
    )(page_table, q, lat_new, kpe_new, wabs, wuv, *([lat_pool] * (nb * pg)), *([pe_pool] * (nb * pg)))


def _merge_kernel(h_ref, gate_ref, b0_ref, b1_ref, b2_ref, b3_ref, wbo_ref, wout_ref, o_ref):
    branches = (b0_ref, b1_ref, b2_ref, b3_ref)
    merged = None
    off = 0
    for i, br in enumerate(branches):
        wd = br.shape[1]
        proj = _dot(br[...].astype(BF16), wbo_ref[off:off + wd, :])
        term = jax.nn.sigmoid(gate_ref[:, i * D_MODEL:(i + 1) * D_MODEL].astype(F32)) * proj
        merged = term if merged is None else merged + term
        off += wd
    o_ref[...] = h_ref[...] + _dot(merged.astype(BF16), wout_ref[...])


def _merge(h, proj, branches, wbo, wout, *, tm):
    n = h.shape[0]
    row = lambda c: pl.BlockSpec((tm, c), lambda i: (i, 0))
    return pl.pallas_call(
        _merge_kernel,
        out_shape=jax.ShapeDtypeStruct((n, D_MODEL), F32),
        grid=(n // tm,),
        in_specs=[row(D_MODEL), row(GATE_W)] + [row(br.shape[1]) for br in branches]
        + [pl.BlockSpec(wbo.shape, lambda i: (0, 0)), pl.BlockSpec(wout.shape, lambda i: (0, 0))],
        out_specs=row(D_MODEL),
        compiler_params=_cparams(("parallel",)),
        name="branch_merge",
    )(h, proj, *branches, wbo, wout)


def _ffn_kernel(h_ref, g_ref, w1_ref, w2_ref, gf_ref, o_ref, *, tf, final_norm):
    h = h_ref[...]
    xn = _rms(h, g_ref[...]).astype(BF16)
    acc = None
    for j in range(w1_ref.shape[1] // tf):
        u = _dot(xn, w1_ref[:, j * tf:(j + 1) * tf])
        u = jnp.square(jnp.maximum(u, 0.0))
        part = _dot(u.astype(BF16), w2_ref[j * tf:(j + 1) * tf, :])
        acc = part if acc is None else acc + part
    out = h + acc
    if final_norm:
        out = _rms(out, gf_ref[...])
    o_ref[...] = out


def _ffn(h, g, w1, w2, gf, *, tm, tf, final_norm):
    n, d = h.shape
    f = w1.shape[1]
    kern = functools.partial(_ffn_kernel, tf=tf, final_norm=final_norm)
    return pl.pallas_call(
        kern,
        out_shape=jax.ShapeDtypeStruct((n, d), F32),
        grid=(n // tm,),
        in_specs=[
            pl.BlockSpec((tm, d), lambda i: (i, 0)),
            pl.BlockSpec((1, d), lambda i: (0, 0)),
            pl.BlockSpec((d, f), lambda i: (0, 0)),
            pl.BlockSpec((f, d), lambda i: (0, 0)),
            pl.BlockSpec((1, d), lambda i: (0, 0)),
        ],
        out_specs=pl.BlockSpec((tm, d), lambda i: (i, 0)),
        compiler_params=_cparams(("parallel",)),
        name="relu2_mlp",
    )(h, g, w1, w2, gf)


def _rot_partner(w, half):
    return jnp.concatenate([-w[..., half:], w[..., :half]], axis=-1)


def _pack_layer(w_in_l, conv_w_l, q_norm_l, w_uq_l, kv_norm_l, w_uk_l, w_uv_l, hgrn_norm_l,
                w_bo_l, w_out_l, g_mix_l, g_ffn_l, w_ff1_l, w_ff2_l):
    d = w_in_l.shape[0]
    offs = np.cumsum((0,) + IN_SIZES)
    col = lambda i: w_in_l[:, int(offs[i]):int(offs[i + 1])]
    rq, rk, rv, rg, cb, cc, cx, mcq, mckv, mkr, hq, hf, hi, hg, mg = [col(i) for i in range(15)]

    def halves(w):
        w4 = w.reshape(d, RET_HEADS, 2, RET_DK // 2)
        return w4[:, :, 0, :].reshape(d, -1), w4[:, :, 1, :].reshape(d, -1)

    rq1, rq2 = halves(rq)
    rk1, rk2 = halves(rk)
    mkr_rot = _rot_partner(mkr, MLA_ROPE // 2)
    pad = jnp.zeros((d, MLA_BLOCK_W - MLA_Q_LORA - MLA_KV_LORA - 2 * MLA_ROPE), w_in_l.dtype)
    packed = jnp.concatenate([mg, rv,
                              hq, hf, hi, hg, rq1, rq2, rk1, rk2, rg, cb, cc, cx,
                              mcq, mckv, mkr, mkr_rot, pad], axis=1).astype(BF16)

    qd = MLA_NOPE + MLA_ROPE
    wq = w_uq_l.reshape(MLA_Q_LORA, MLA_HEADS, qd)
    wq_nope = jnp.concatenate(
        [wq[:, :, :MLA_NOPE], jnp.zeros((MLA_Q_LORA, MLA_HEADS, MLA_HEAD_PAD - MLA_NOPE), wq.dtype)],
        axis=2).reshape(MLA_Q_LORA, MLA_HEADS * MLA_HEAD_PAD)
    wq_pe = wq[:, :, MLA_NOPE:]
    wq_rot = _rot_partner(wq_pe, MLA_ROPE // 2)
    e_q = np.zeros((MLA_HEADS * MLA_ROPE, MLA_HEADS * MLA_HEAD_PAD), np.float32)
    e_k = np.zeros((MLA_ROPE, MLA_HEADS * MLA_HEAD_PAD), np.float32)
    for h in range(MLA_HEADS):
        for i in range(MLA_ROPE):
            e_q[h * MLA_ROPE + i, h * MLA_HEAD_PAD + MLA_NOPE + i] = 1.0
            e_k[i, h * MLA_HEAD_PAD + MLA_NOPE + i] = 1.0
    wuk_pad = jnp.concatenate(
        [w_uk_l, jnp.zeros((MLA_KV_LORA, MLA_HEADS, MLA_HEAD_PAD - MLA_NOPE), w_uk_l.dtype)],
        axis=2).reshape(MLA_KV_LORA, MLA_HEADS * MLA_HEAD_PAD)
    wabs = jnp.zeros((MLA_HEADS, MLA_HEAD_PAD, MLA_KV_LORA + LANES), F32)
    wabs = wabs.at[:, :MLA_NOPE, :MLA_KV_LORA].set(jnp.transpose(w_uk_l, (1, 2, 0)))
    wabs = wabs.at[:, MLA_NOPE:MLA_NOPE + MLA_ROPE, MLA_KV_LORA:MLA_KV_LORA + MLA_ROPE].set(
        jnp.broadcast_to(jnp.eye(MLA_ROPE, dtype=F32), (MLA_HEADS, MLA_ROPE, MLA_ROPE)))
    wabs = wabs.reshape(MLA_HEADS * MLA_HEAD_PAD, MLA_KV_LORA + LANES)
    return {
        "w_in": packed,
        "g_mix": g_mix_l.reshape(1, d),
        "conv_w": conv_w_l,
        "q_norm": q_norm_l.reshape(1, -1),
        "kv_norm": kv_norm_l.reshape(1, -1),
        "wq_nope": wq_nope.astype(BF16),
        "wq_pe": wq_pe.reshape(MLA_Q_LORA, -1).astype(BF16),
        "wq_rot": wq_rot.reshape(MLA_Q_LORA, -1).astype(BF16),
        "e_q": jnp.asarray(e_q, BF16),
        "e_k": jnp.asarray(e_k, BF16),
        "wuk_pad": wuk_pad.astype(BF16),
        "wuv": w_uv_l.reshape(MLA_KV_LORA, MLA_W).astype(BF16),
        "wuv_t": w_uv_l.reshape(MLA_KV_LORA, MLA_W).T.astype(BF16),
        "wabs": wabs.astype(BF16),
        "hgrn_norm": hgrn_norm_l.reshape(1, -1),
        "w_bo": w_bo_l.astype(BF16),
        "w_out": w_out_l.astype(BF16),
        "g_ffn": g_ffn_l.reshape(1, d),
        "w_ff1": w_ff1_l.astype(BF16),
        "w_ff2": w_ff2_l.astype(BF16),
    }


def _rope_tables(pos):
    def cs(half):
        inv = ROPE_THETA ** (-jnp.arange(half, dtype=F32) / half)
        ang = pos.astype(F32)[:, None] * inv[None, :]
        return jnp.cos(ang), jnp.sin(ang)

    c_ret, s_ret = cs(RET_DK // 2)
    c_pe, s_pe = cs(MLA_ROPE // 2)
    two = lambda a: jnp.concatenate([a, a], axis=1)
    return {
        "cos_ret": jnp.tile(c_ret, (1, RET_HEADS)), "sin_ret": jnp.tile(s_ret, (1, RET_HEADS)),
        "cos_q": jnp.tile(two(c_pe), (1, MLA_HEADS)), "sin_q": jnp.tile(two(s_pe), (1, MLA_HEADS)),
        "cos_k": two(c_pe), "sin_k": two(s_pe),
    }


def _tiles(t):
    return {"conv": min(512, t), "hgrn": min(256, t), "prep": 512, "tq": min(512, t), "tk": min(512, t)}


def _trunk_layer(h, tabs, ret_state, conv_buf, hgrn_state, lb, lw, *, b, t, attend, final_gain):
    n = b * t
    tl = _tiles(t)
    odt = BF16 if t % 16 == 0 else F32
    proj_a, proj_b = _inproj(h, lw["g_mix"], lw["w_in"], tm=min(n, 2048), adt=odt)
    o_ret, ret_new = _retention(proj_a, proj_b, tabs["cos_ret"], tabs["sin_ret"], ret_state, b=b, t=t, odt=odt)
    o_conv, conv_new = _short_conv(proj_b, lw["conv_w"], conv_buf, b=b, t=t, rows=tl["conv"], odt=odt)
    o_h, hgrn_new = _hgrn(proj_b, lb, lw["hgrn_norm"], hgrn_state, b=b, t=t, rows=tl["hgrn"], odt=odt)
    o_mla, ckv, kpe = attend(proj_b, lw, tabs, tl, odt)
    h = _merge(h, proj_a, (o_ret, o_conv, o_mla, o_h), lw["w_bo"], lw["w_out"], tm=min(n, 512))
    gf = final_gain if final_gain is not None else lw["g_ffn"]
    h = _ffn(h, lw["g_ffn"], lw["w_ff1"], lw["w_ff2"], gf, tm=min(n, 512), tf=512,
             final_norm=final_gain is not None)
    return h, ret_new, conv_new, hgrn_new, ckv, kpe


def kernel(x_prompt, x_sample, cache_mla_latent, cache_mla_rope, page_table,
           state_retention, state_conv, state_hgrn,
           norm_mix, w_in, conv_w, mla_q_norm, mla_w_uq, mla_kv_norm, mla_w_uk, mla_w_uv,
           hgrn_lb_logits, hgrn_out_norm, w_branch_out, w_out, norm_ffn, w_ff1, w_ff2, norm_final):
    bp, tp, d = x_prompt.shape
    bs, ts, _ = x_sample.shape
    depth = w_in.shape[0]
    past_len = page_table.shape[1] * PAGE_SIZE
    tabs_p = _rope_tables(jnp.arange(tp, dtype=jnp.int32))
    tabs_s = _rope_tables(past_len + jnp.arange(ts, dtype=jnp.int32))
    lb_p = jax.nn.softmax(hgrn_lb_logits.astype(F32), axis=0)
    lower_bounds = jnp.cumsum(lb_p, axis=0) - lb_p[0:1]
    gfin = norm_final.reshape(1, d)
    rope_pool_t = jnp.swapaxes(cache_mla_rope, 2, 3)

    hp = x_prompt.reshape(bp * tp, d)
    hs = x_sample.reshape(bs * ts, d)
    zeros_ret = jnp.zeros((bp, RET_HEADS, RET_DK, RET_DV), F32)
    zeros_conv = jnp.zeros((bp, CONV_K - 1, CONV_WIDTH), F32)
    zeros_hgrn = jnp.zeros((bp, HGRN_HEADS, HGRN_DK, HGRN_DV), F32)
    outs = {k: [] for k in ("lat_p", "rope_p", "lat_s", "rope_s", "ret_p", "ret_s",
                            "conv_p", "conv_s", "hg_p", "hg_s")}
    for l in range(depth):
        lw = _pack_layer(w_in[l], conv_w[l], mla_q_norm[l], mla_w_uq[l], mla_kv_norm[l], mla_w_uk[l],
                         mla_w_uv[l], hgrn_out_norm[l], w_branch_out[l], w_out[l], norm_mix[l],
                         norm_ffn[l], w_ff1[l], w_ff2[l])
        lb = lower_bounds[l].reshape(1, HGRN_W)
        last = gfin if l == depth - 1 else None

        def attend_prompt(proj, lw, tabs, tl, odt):
            q, ckv, kpe, kc, v = _mla_prep(proj, lw, tabs, n=bp * tp, t=tp, tm=tl["prep"], with_kv=True)
            o = _flash(q, kc, v, b=bp, t=tp, tq=tl["tq"], tk=tl["tk"], odt=odt)
            return o, ckv, kpe

        def attend_sample(proj, lw, tabs, tl, odt, layer=l):
            q, ckv, kpe = _mla_prep(proj, lw, tabs, n=bs * ts, t=ts, tm=tl["prep"], with_kv=False)
            o = _paged_attention(page_table, q.reshape(bs, ts, -1), ckv.reshape(bs, ts, -1),
                                 kpe.reshape(bs, ts, -1), lw["wabs"], lw["wuv"],
                                 cache_mla_latent, rope_pool_t, layer=layer,
                                 pg=math.gcd(16, page_table.shape[1]), nb=math.gcd(2, bs))
            return o.reshape(bs * ts, MLA_W), ckv, kpe

        hp, r, c, g, ckv, kpe = _trunk_layer(hp, tabs_p, zeros_ret, zeros_conv, zeros_hgrn, lb, lw,
                                             b=bp, t=tp, attend=attend_prompt, final_gain=last)
        outs["ret_p"].append(r); outs["conv_p"].append(c); outs["hg_p"].append(g)
        outs["lat_p"].append(ckv.reshape(bp, tp, -1)); outs["rope_p"].append(kpe.reshape(bp, tp, -1))
        hs, r, c, g, ckv, kpe = _trunk_layer(hs, tabs_s, state_retention[l], state_conv[l], state_hgrn[l],
                                             lb, lw, b=bs, t=ts, attend=attend_sample, final_gain=last)
        outs["ret_s"].append(r); outs["conv_s"].append(c); outs["hg_s"].append(g)
        outs["lat_s"].append(ckv.reshape(bs, ts, -1)); outs["rope_s"].append(kpe.reshape(bs, ts, -1))

    st = lambda k: jnp.stack(outs[k])
    return (hp.reshape(bp, tp, d), hs.reshape(bs, ts, d),
            st("lat_p"), st("rope_p"), st("lat_s"), st("rope_s"),
            st("ret_p"), st("ret_s"), st("conv_p"), st("conv_s"), st("hg_p"), st("hg_s"))
```
